```python
import functools
import jax
import jax.numpy as jnp
from jax import lax
import numpy as np

D_MODEL = 1024
BATCH = 16
SEQ = 4096
DEPTH = 1
DEC_BATCH = 128
DEC_SEQ = 1
PAST_LEN = 8192
PAGE_SIZE = 128

HEAD_DIM = 64
NSA_HEADS = 8
NSA_GROUPS = 2
NSA_REP = NSA_HEADS // NSA_GROUPS
SB_HEADS = 8
CMP_LEN = 32
CMP_STRIDE = 16
CMP_HIDDEN = 128
SEL_LEN = 64
SEL_TOPK = 16
WINDOW = 512
Q_BLOCK = 128
D_FF = 2816
CONV_W = 3
RMS_EPS = 1e-6
NEG_INF = -1e30
FORCE_BONUS = 1e6
ATTN_SCALE = HEAD_DIM ** -0.5
BLK_PER_PAGE = PAGE_SIZE // SEL_LEN

R_CMP_K = 0
R_CMP_V = NSA_GROUPS
R_SEL_K = 2 * NSA_GROUPS
R_SEL_V = 3 * NSA_GROUPS
R_SB_K = 4 * NSA_GROUPS
R_SB_V = 4 * NSA_GROUPS + SB_HEADS
KV_ROWS = 4 * NSA_GROUPS + 2 * SB_HEADS
WIN_ROWS = 2 * NSA_GROUPS

Q_NSA_COLS = NSA_HEADS * HEAD_DIM
KV_COLS = KV_ROWS * HEAD_DIM
WIN_COLS = WIN_ROWS * HEAD_DIM
Q_SB_COLS = SB_HEADS * HEAD_DIM
NSA_GATE_COLS = 3 * NSA_HEADS
MERGE_COLS = 2 * D_MODEL
PROJ_COLS = Q_NSA_COLS + KV_COLS + WIN_COLS + Q_SB_COLS + NSA_GATE_COLS + MERGE_COLS

kernel_name = 'nsa_stickbreaking_parallel_decoder_step'


def rms_norm(x, g):
    xf = x.astype(jnp.float32)
    y = xf * lax.rsqrt(jnp.mean(xf * xf, axis=-1, keepdims=True) + RMS_EPS)
    return (y * g.astype(jnp.float32)).astype(x.dtype)


def alibi_slopes():
    h = np.arange(1, NSA_HEADS + 1, dtype=np.float32)
    s = np.power(np.float32(2.0), -8.0 * h / NSA_HEADS).astype(np.float32)
    return jnp.asarray(s).reshape(NSA_GROUPS, NSA_REP)


def masked_softmax(s, valid):
    s = jnp.where(valid, s, NEG_INF)
    m = jnp.max(s, axis=-1, keepdims=True)
    p = jnp.where(valid, jnp.exp(s - m), 0.0)
    return p / jnp.maximum(jnp.sum(p, axis=-1, keepdims=True), 1e-30)


def split_projection(z):
    b, t = z.shape[:2]
    o1 = Q_NSA_COLS
    o2 = o1 + KV_COLS
    o3 = o2 + WIN_COLS
    o4 = o3 + Q_SB_COLS
    o5 = o4 + NSA_GATE_COLS
    q_nsa = z[..., :o1].reshape(b, t, NSA_HEADS, HEAD_DIM)
    kv = z[..., o1:o2].reshape(b, t, KV_ROWS, HEAD_DIM)
    win = z[..., o2:o3].reshape(b, t, WIN_ROWS, HEAD_DIM)
    q_sb = z[..., o3:o4].reshape(b, t, SB_HEADS, HEAD_DIM)
    nsa_gate = jax.nn.sigmoid(z[..., o4:o5]).reshape(b, t, NSA_HEADS, 3)
    merge_gate = jax.nn.sigmoid(z[..., o5:]).reshape(b, t, 2, D_MODEL)
    return q_nsa, kv, win, q_sb, nsa_gate, merge_gate


def compress_kv(k, v, w1, w2, pe):
    t = k.shape[1]
    n_cmp = (t - CMP_LEN) // CMP_STRIDE + 1
    idx = np.arange(n_cmp)[:, None] * CMP_STRIDE + np.arange(CMP_LEN)[None, :]
    blocks = jnp.stack([k, v], 0)[:, :, idx]
    blocks = blocks + pe[:, None, None, :, None, :]
    hid = jax.nn.gelu(jnp.einsum('zbnlgc,zlch->zbngh', blocks, w1))
    out = jnp.einsum('zbngh,zhc->zbngc', hid, w2)
    return out[0], out[1], idx[:, -1].astype(np.int32)


def overlap_matrix(n_cmp, n_sel):
    cs = np.arange(n_cmp) * CMP_STRIDE
    ss = np.arange(n_sel) * SEL_LEN
    m = (cs[:, None] < ss[None, :] + SEL_LEN) & (cs[:, None] + CMP_LEN > ss[None, :])
    return jnp.asarray(m.astype(np.float32))


def nsa_compressed_attn(q, q_pos, kc, vc, c_end, slopes):
    b, tq = q.shape[:2]
    qg = q.reshape(b, tq, NSA_GROUPS, NSA_REP, HEAD_DIM)
    s = jnp.einsum('btgrd,bngd->btgrn', qg, kc).astype(jnp.float32) * ATTN_SCALE
    d = q_pos[:, None] - jnp.asarray(c_end)[None, :]
    dist = d.astype(jnp.float32)[None, :, None, None, :]
    s = s - slopes[None, None, :, :, None] * dist
    p = masked_softmax(s, (d >= 0)[None, :, None, None, :])
    o = jnp.einsum('btgrn,bngd->btgrd', p.astype(vc.dtype), vc)
    return o.reshape(b, tq, NSA_HEADS, HEAD_DIM), p


def nsa_select_blocks(p_cmp, q_pos, overlap):
    n_sel = overlap.shape[1]
    ps = jnp.einsum('btgrn,ns->btgs', p_cmp, overlap)
    j = jnp.arange(n_sel)[None, :]
    cur = (q_pos // SEL_LEN)[:, None]
    valid = j <= cur
    forced = (j == 0) | (j == cur) | (j == cur - 1)
    bonus = jnp.where(forced, FORCE_BONUS, 0.0)
    score = jnp.where(valid[None, :, None, :], ps + bonus[None, :, None, :], NEG_INF)
    _, idx = lax.top_k(score, min(SEL_TOPK, n_sel))
    return idx


def nsa_selected_attn(q, q_pos, ks, vs, kpos, slopes):
    b, tq = q.shape[:2]
    n = ks.shape[3] * SEL_LEN
    ks = ks.reshape(b, tq, NSA_GROUPS, n, HEAD_DIM)
    vs = vs.reshape(b, tq, NSA_GROUPS, n, HEAD_DIM)
    qg = q.reshape(b, tq, NSA_GROUPS, NSA_REP, HEAD_DIM)
    s = jnp.einsum('btgrd,btgnd->btgrn', qg, ks).astype(jnp.float32) * ATTN_SCALE
    d = q_pos[None, :, None, None] - kpos.reshape(b, tq, NSA_GROUPS, n)
    dist = d.astype(jnp.float32)[:, :, :, None, :]
    s = s - slopes[None, None, :, :, None] * dist
    p = masked_softmax(s, (d >= 0)[:, :, :, None, :])
    o = jnp.einsum('btgrn,btgnd->btgrd', p.astype(vs.dtype), vs)
    return o.reshape(b, tq, NSA_HEADS, HEAD_DIM)


def nsa_window_attn(q, q_pos, k, v, k_pos, slopes):
    b, tq = q.shape[:2]
    qg = q.reshape(b, tq, NSA_GROUPS, NSA_REP, HEAD_DIM)
    s = jnp.einsum('btgrd,bkgd->btgrk', qg, k).astype(jnp.float32) * ATTN_SCALE
    d = q_pos[:, None] - k_pos[None, :]
    dist = d.astype(jnp.float32)[None, :, None, None, :]
    s = s - slopes[None, None, :, :, None] * dist
    valid = (d >= 0) & (d < WINDOW) & (k_pos >= 0)[None, :]
    p = masked_softmax(s, valid[None, :, None, None, :])
    o = jnp.einsum('btgrk,bkgd->btgrd', p.astype(v.dtype), v)
    return o.reshape(b, tq, NSA_HEADS, HEAD_DIM)


def stick_breaking_attn(q, q_pos, k, v, k_pos):
    z = jnp.einsum('bthd,bshd->bths', q, k).astype(jnp.float32) * ATTN_SCALE
    past = (k_pos[None, :] < q_pos[:, None])[None, :, None, :]
    log_keep = jnp.where(past, jax.nn.log_sigmoid(-z), 0.0)
    log_rest = lax.cumsum(log_keep, axis=3, reverse=True) - log_keep
    a = jnp.where(past, jnp.exp(jax.nn.log_sigmoid(z) + log_rest), 0.0)
    return jnp.einsum('bths,bshd->bthd', a.astype(v.dtype), v)


def mix_block(q_a, q_b, gate, q_pos, kc, vc, c_end, overlap, gather_sel,
              kw, vw, kw_pos, k_sb, v_sb, sb_pos, slopes):
    o_cmp, p_cmp = nsa_compressed_attn(q_a, q_pos, kc, vc, c_end, slopes)
    idx = nsa_select_blocks(p_cmp, q_pos, overlap)
    ks, vs = gather_sel(idx)
    kpos = idx[..., None] * SEL_LEN + jnp.arange(SEL_LEN)
    o_sel = nsa_selected_attn(q_a, q_pos, ks, vs, kpos, slopes)
    o_win = nsa_window_attn(q_a, q_pos, kw, vw, kw_pos, slopes)
    o_a = gate[..., 0:1] * o_cmp + gate[..., 1:2] * o_sel + gate[..., 2:3] * o_win
    o_b = stick_breaking_attn(q_b, q_pos, k_sb, v_sb, sb_pos)
    return o_a, o_b


def mixers_prompt(q_nsa, kv, win, q_sb, nsa_gate, cmp_w1, cmp_w2, cmp_pe, wbuf):
    b, t = q_nsa.shape[:2]
    g = NSA_GROUPS
    slopes = alibi_slopes()
    kc, vc, c_end = compress_kv(kv[:, :, R_CMP_K:R_CMP_K + g], kv[:, :, R_CMP_V:R_CMP_V + g],
                                cmp_w1, cmp_w2, cmp_pe)
    n_sel = t // SEL_LEN
    overlap = overlap_matrix(c_end.shape[0], n_sel)
    sel_blk = kv[:, :, R_SEL_K:R_SEL_V + g].reshape(b, n_sel, SEL_LEN, 2 * g, HEAD_DIM).transpose(0, 3, 1, 2, 4)
    win_pad = jnp.pad(win, ((0, 0), (WINDOW, 0), (0, 0), (0, 0)))
    sb_k = kv[:, :, R_SB_K:R_SB_K + SB_HEADS]
    sb_v = kv[:, :, R_SB_V:R_SB_V + SB_HEADS]
    sb_pos = jnp.arange(t)
    gi = np.arange(g)[None, None, :, None]

    def per_seq(args):
        qn, qs, gate, kc1, vc1, sblk, wpad, sk, sv = args

        def gather_sel(idx):
            return sblk[gi, idx], sblk[g + gi, idx]

        def per_block(blk):
            t0 = blk * Q_BLOCK
            q_pos = t0 + jnp.arange(Q_BLOCK)

            def rows(a):
                return lax.dynamic_slice_in_dim(a, t0, Q_BLOCK, axis=0)[None]

            band = lax.dynamic_slice_in_dim(wpad, t0, Q_BLOCK + WINDOW, axis=0)[None]
            band_pos = t0 - WINDOW + jnp.arange(Q_BLOCK + WINDOW)
            o_a, o_b = mix_block(rows(qn), rows(qs), rows(gate), q_pos, kc1[None], vc1[None], c_end,
                                 overlap, gather_sel, band[:, :, :g], band[:, :, g:], band_pos,
                                 sk[None], sv[None], sb_pos, slopes)
            return o_a[0], o_b[0]

        o_a, o_b = lax.map(per_block, jnp.arange(t // Q_BLOCK))
        return o_a.reshape(t, NSA_HEADS, HEAD_DIM), o_b.reshape(t, SB_HEADS, HEAD_DIM)

    o_a, o_b = lax.map(per_seq, (q_nsa, q_sb, nsa_gate, kc, vc, sel_blk, win_pad, sb_k, sb_v))
    win_state = jnp.pad(win, ((0, 0), (wbuf, 0), (0, 0), (0, 0)))[:, -wbuf:]
    return o_a, o_b, win_state


def mixers_sample(q_nsa, kv, win, q_sb, nsa_gate, cmp_w1, cmp_w2, cmp_pe, pool, page_table, win_buf):
    s = q_nsa.shape[1]
    g = NSA_GROUPS
    n_pages = page_table.shape[1]
    past_len = n_pages * PAGE_SIZE
    total = past_len + s
    wbuf = win_buf.shape[1]
    n_sel = -(-total // SEL_LEN)
    n_past_blk = past_len // SEL_LEN
    n_new_blk = n_sel - n_past_blk
    slopes = alibi_slopes()
    q_pos = past_len + jnp.arange(s)
    key_pos = jnp.arange(total)
    win_pos = past_len - wbuf + jnp.arange(wbuf + s)
    dense_rows = np.concatenate([np.arange(R_CMP_K, R_CMP_V + g), np.arange(R_SB_K, KV_ROWS)])
    pool_blk = pool.reshape(pool.shape[0], BLK_PER_PAGE, SEL_LEN, KV_ROWS, HEAD_DIM)
    gi = np.arange(g)[None, None, :, None]

    def per_seq(args):
        qn, kvn, wn, qs, gate, pages, wb = args
        past = pool[pages[:, None, None], np.arange(PAGE_SIZE)[None, :, None], dense_rows[None, None, :]]
        dense = jnp.concatenate([past.reshape(past_len, dense_rows.shape[0], HEAD_DIM),
                                 kvn[:, dense_rows]], axis=0)[None]
        kc, vc, c_end = compress_kv(dense[:, :, 0:g], dense[:, :, g:2 * g], cmp_w1, cmp_w2, cmp_pe)
        overlap = overlap_matrix(c_end.shape[0], n_sel)
        sb_k = dense[:, :, 2 * g:2 * g + SB_HEADS]
        sb_v = dense[:, :, 2 * g + SB_HEADS:]
        new_blk = jnp.pad(kvn, ((0, n_new_blk * SEL_LEN - s), (0, 0), (0, 0))).reshape(
            n_new_blk, SEL_LEN, KV_ROWS, HEAD_DIM)

        def gather_sel(idx):
            in_past = (idx < n_past_blk)[..., None, None]
            jp = jnp.minimum(idx, n_past_blk - 1)
            phys = pages[jp // BLK_PER_PAGE]
            sub = jp % BLK_PER_PAGE
            jn = jnp.clip(idx - n_past_blk, 0, n_new_blk - 1)

            def rows(r0):
                return jnp.where(in_past, pool_blk[phys, sub, :, r0 + gi], new_blk[jn, :, r0 + gi])

            return rows(R_SEL_K), rows(R_SEL_V)

        wkeys = jnp.concatenate([wb, wn], axis=0)
        o_a, o_b = mix_block(qn[None], qs[None], gate[None], q_pos, kc, vc, c_end, overlap, gather_sel,
                             wkeys[None, :, :g], wkeys[None, :, g:], win_pos, sb_k, sb_v, key_pos, slopes)
        return o_a[0], o_b[0], wkeys[s:]

    return lax.map(per_seq, (q_nsa, kv, win, q_sb, nsa_gate, page_table, win_buf))


def conv_ffn(h, prev, w_up, conv_w, conv_b, w_down):
    t = h.shape[1]
    u = h @ w_up
    ext = jnp.concatenate([prev.astype(u.dtype), u], axis=1)
    c = conv_b + ext[:, 0:t] * conv_w[0]
    for k in range(1, CONV_W):
        c = c + ext[:, k:k + t] * conv_w[k]
    a, b = jnp.split(c, 2, axis=-1)
    y = (jax.nn.gelu(a, approximate=True) * b) @ w_down
    return y, ext[:, t:]


def decoder_layer(x, mix_fn, conv_prev, norm_g, w_in, w_branch_a, w_branch_b, w_out,
                  ffn_w_up, ffn_conv_w, ffn_conv_b, ffn_w_down):
    b, t = x.shape[:2]
    h = rms_norm(x, norm_g[0])
    q_nsa, kv, win, q_sb, nsa_gate, merge_gate = split_projection(h @ w_in)
    o_a, o_b, win_state = mix_fn(q_nsa, kv, win, q_sb, nsa_gate)
    u_a = o_a.reshape(b, t, Q_NSA_COLS) @ w_branch_a
    u_b = o_b.reshape(b, t, Q_SB_COLS) @ w_branch_b
    mixed = (merge_gate[:, :, 0] * u_a + merge_gate[:, :, 1] * u_b) @ w_out
    x = x + rms_norm(mixed, norm_g[1])
    f, conv_state = conv_ffn(rms_norm(x, norm_g[2]), conv_prev, ffn_w_up, ffn_conv_w, ffn_conv_b, ffn_w_down)
    x = x + rms_norm(f, norm_g[3])
    return x, kv, win_state, conv_state


def _paged_normal(key, n_pages_total, row_shape):
    keys = jax.random.split(key, n_pages_total)
    return lax.map(lambda k: jax.random.normal(k, row_shape, jnp.float32), keys)


def setup_inputs(seed: int = 0) -> dict:
    key = jax.random.key(seed)
    ks = jax.random.split(key, 18)
    n_pages = PAST_LEN // PAGE_SIZE
    n_used = DEC_BATCH * n_pages
    n_pool = n_used + (n_used + 3) // 4
    wbuf = min(WINDOW, PAST_LEN)

    def nrm(k, shape, scale):
        return jax.random.normal(k, shape, jnp.float32) * scale

    cache_kv = _paged_normal(ks[2], DEPTH * n_pool, (PAGE_SIZE, KV_ROWS, HEAD_DIM)).reshape(
        DEPTH, n_pool, PAGE_SIZE, KV_ROWS, HEAD_DIM)
    page_table = jax.random.permutation(ks[5], n_pool)[:n_used].reshape(DEC_BATCH, n_pages).astype(jnp.int32)
    return {
        'x_prompt': nrm(ks[0], (BATCH, SEQ, D_MODEL), 1.0),
        'x_sample': nrm(ks[1], (DEC_BATCH, DEC_SEQ, D_MODEL), 1.0),
        'cache_kv': cache_kv,
        'cache_win': nrm(ks[3], (DEPTH, DEC_BATCH, wbuf, WIN_ROWS, HEAD_DIM), 1.0),
        'state_conv': nrm(ks[4], (DEPTH, DEC_BATCH, CONV_W - 1, 2 * D_FF), 1.0),
        'page_table': page_table,
        'norm_g': 1.0 + nrm(ks[6], (DEPTH, 4, D_MODEL), 0.05),
        'w_in': nrm(ks[7], (DEPTH, D_MODEL, PROJ_COLS), D_MODEL ** -0.5),
        'cmp_w1': nrm(ks[8], (DEPTH, 2, CMP_LEN, HEAD_DIM, CMP_HIDDEN), (CMP_LEN * HEAD_DIM) ** -0.5),
        'cmp_w2': nrm(ks[9], (DEPTH, 2, CMP_HIDDEN, HEAD_DIM), CMP_HIDDEN ** -0.5),
        'cmp_pe': nrm(ks[10], (DEPTH, 2, CMP_LEN, HEAD_DIM), 0.1),
        'w_branch_a': nrm(ks[11], (DEPTH, Q_NSA_COLS, D_MODEL), Q_NSA_COLS ** -0.5),
        'w_branch_b': nrm(ks[12], (DEPTH, Q_SB_COLS, D_MODEL), Q_SB_COLS ** -0.5),
        'w_out': nrm(ks[13], (DEPTH, D_MODEL, D_MODEL), D_MODEL ** -0.5),
        'ffn_w_up': nrm(ks[14], (DEPTH, D_MODEL, 2 * D_FF), D_MODEL ** -0.5),
        'ffn_conv_w': nrm(ks[15], (DEPTH, CONV_W, 2 * D_FF), CONV_W ** -0.5),
        'ffn_conv_b': nrm(ks[16], (DEPTH, 2 * D_FF), 0.01),
        'ffn_w_down': nrm(ks[17], (DEPTH, D_FF, D_MODEL), D_FF ** -0.5),
    }


def reference(x_prompt, x_sample, cache_kv, cache_win, state_conv, page_table,
              norm_g, w_in, cmp_w1, cmp_w2, cmp_pe, w_branch_a, w_branch_b, w_out,
              ffn_w_up, ffn_conv_w, ffn_conv_b, ffn_w_down):
    wbuf = cache_win.shape[2]
    xp, xs = x_prompt, x_sample
    kv_p, kv_s, win_p, win_s, conv_p, conv_s = [], [], [], [], [], []
    for l in range(DEPTH):
        shared = (norm_g[l], w_in[l], w_branch_a[l], w_branch_b[l], w_out[l],
                  ffn_w_up[l], ffn_conv_w[l], ffn_conv_b[l], ffn_w_down[l])
        mix_p = functools.partial(mixers_prompt, cmp_w1=cmp_w1[l], cmp_w2=cmp_w2[l], cmp_pe=cmp_pe[l], wbuf=wbuf)
        conv0 = jnp.zeros((xp.shape[0], CONV_W - 1, 2 * D_FF), xp.dtype)
        xp, kv1, win1, conv1 = decoder_layer(xp, mix_p, conv0, *shared)
        mix_s = functools.partial(mixers_sample, cmp_w1=cmp_w1[l], cmp_w2=cmp_w2[l], cmp_pe=cmp_pe[l],
                                  pool=cache_kv[l], page_table=page_table, win_buf=cache_win[l])
        xs, kv2, win2, conv2 = decoder_layer(xs, mix_s, state_conv[l], *shared)
        kv_p.append(kv1)
        kv_s.append(kv2)
        win_p.append(win1)
        win_s.append(win2)
        conv_p.append(conv1)
        conv_s.append(conv2)
    return (xp, xs, jnp.stack(kv_p), jnp.stack(kv_s), jnp.stack(win_p), jnp.stack(win_s),
            jnp.stack(conv_p), jnp.stack(conv_s))
```

```python
import functools
import math

import numpy as np
import jax
import jax.numpy as jnp
from jax import lax
from jax.experimental import pallas as pl
from jax.experimental.pallas import tpu as pltpu

HEAD_DIM = 64
NSA_HEADS = 8
NSA_GROUPS = 2
NSA_REP = NSA_HEADS // NSA_GROUPS
SB_HEADS = 8
CMP_LEN = 32
CMP_STRIDE = 16
CMP_HIDDEN = 128
SEL_LEN = 64
SEL_TOPK = 16
WINDOW = 512
CONV_W = 3
RMS_EPS = 1e-6
NEG_INF = -1e30
FORCE_BONUS = 1e6
ATTN_SCALE = HEAD_DIM ** -0.5
KV_ROWS = 4 * NSA_GROUPS + 2 * SB_HEADS
WIN_ROWS = 2 * NSA_GROUPS

LANES = 128
Q_NSA_COLS = NSA_HEADS * HEAD_DIM
KV_COLS = KV_ROWS * HEAD_DIM
WIN_COLS = WIN_ROWS * HEAD_DIM
Q_SB_COLS = SB_HEADS * HEAD_DIM
NSA_GATE_COLS = 3 * NSA_HEADS
QN_PAD_COLS = NSA_HEADS * LANES

BLK_CMP_K, BLK_CMP_V, BLK_SEL_K, BLK_SEL_V = 0, 1, 2, 3
BLK_SB_K = 4
BLK_SB_V = 4 + SB_HEADS // 2

VMEM_LIMIT = 56 * 1024 * 1024

_SLOPES = [float(np.power(np.float32(2.0), np.float32(-8.0 * (h + 1) / NSA_HEADS))) for h in range(NSA_HEADS)]


def _dot(a, b):
    return jnp.dot(a, b, preferred_element_type=jnp.float32)


def _dot_nt(a, b):
    return lax.dot_general(a, b, (((1,), (1,)), ((), ())), preferred_element_type=jnp.float32)


def _dot_split(x, w):
    hi = x.astype(jnp.bfloat16)
    lo = (x - hi.astype(jnp.float32)).astype(jnp.bfloat16)
    return _dot(hi, w) + _dot(lo, w)


def _rms(x, g):
    return x * lax.rsqrt(jnp.mean(x * x, axis=-1, keepdims=True) + RMS_EPS) * g


def _gelu_tanh(x):
    return 0.5 * x * (1.0 + jnp.tanh(math.sqrt(2.0 / math.pi) * (x + 0.044715 * (x * x * x))))


def _sigmoid(x):
    return 1.0 / (1.0 + jnp.exp(-x))


def _iota(shape, dim):
    return lax.broadcasted_iota(jnp.int32, shape, dim)


def _slope_col():
    hrow = _iota((NSA_HEADS, 1), 0)
    out = jnp.zeros((NSA_HEADS, 1), jnp.float32)
    for h in range(NSA_HEADS):
        out = jnp.where(hrow == h, _SLOPES[h], out)
    return out


def _params(*sem):
    return pltpu.CompilerParams(dimension_semantics=sem, vmem_limit_bytes=VMEM_LIMIT)


_SEG_QN = (0, QN_PAD_COLS)
_SEG_KV = (_SEG_QN[1], _SEG_QN[1] + KV_COLS)
_SEG_WIN = (_SEG_KV[1], _SEG_KV[1] + WIN_COLS)
_SEG_QS = (_SEG_WIN[1], _SEG_WIN[1] + Q_SB_COLS)
_SEG_NG = (_SEG_QS[1], _SEG_QS[1] + LANES)
PROJ_PAD_COLS_FIXED = _SEG_NG[1]


def _proj_weights(w_in, d_model):
    o1 = Q_NSA_COLS
    o2 = o1 + KV_COLS
    o3 = o2 + WIN_COLS
    o4 = o3 + Q_SB_COLS
    o5 = o4 + NSA_GATE_COLS
    wq = (w_in[:, :o1] * ATTN_SCALE).reshape(d_model, NSA_GROUPS, NSA_REP, 1, HEAD_DIM)
    half = (jnp.arange(NSA_GROUPS)[:, None] == jnp.arange(2)[None, :]).astype(w_in.dtype)
    wq = (wq * half[None, :, None, :, None]).reshape(d_model, QN_PAD_COLS)
    ng = jnp.pad(w_in[:, o4:o5], ((0, 0), (0, LANES - NSA_GATE_COLS)))
    w = jnp.concatenate([wq, w_in[:, o1:o3], w_in[:, o3:o4] * ATTN_SCALE, ng, w_in[:, o5:]], axis=1)
    return w.astype(jnp.bfloat16)


def _proj_body(x_ref, g_ref, w_ref, qn_ref, kv_ref, kvb_ref, win_ref, winb_ref, qs_ref, ng_ref, mg_ref):
    h = _rms(x_ref[...], g_ref[...]).astype(jnp.bfloat16)

    def seg(c0, c1, fn, step=512):
        for a in range(c0, c1, step):
            b = min(a + step, c1)
            fn(a - c0, b - c0, _dot(h, w_ref[:, a:b]))

    def put_qn(a, b, r):
        qn_ref[:, a:b] = r.astype(qn_ref.dtype)

    def put_kv(a, b, r):
        kv_ref[:, a:b] = r
        kvb_ref[:, a:b] = r.astype(kvb_ref.dtype)

    def put_win(a, b, r):
        win_ref[:, a:b] = r
        winb_ref[:, a:b] = r.astype(winb_ref.dtype)

    def put_qs(a, b, r):
        qs_ref[:, a:b] = r.astype(qs_ref.dtype)

    def put_ng(a, b, r):
        ng_ref[:, a:b] = _sigmoid(r)

    def put_mg(a, b, r):
        mg_ref[:, a:b] = _sigmoid(r).astype(mg_ref.dtype)

    seg(*_SEG_QN, put_qn)
    seg(*_SEG_KV, put_kv)
    seg(*_SEG_WIN, put_win)
    seg(*_SEG_QS, put_qs)
    seg(*_SEG_NG, put_ng)
    seg(PROJ_PAD_COLS_FIXED, w_ref.shape[1], put_mg)


def _proj(x2, g, w):
    m, d = x2.shape
    tm = min(256, m)
    n = w.shape[1]
    n_mg = n - PROJ_PAD_COLS_FIXED

    def rows(c):
        return pl.BlockSpec((tm, c), lambda i: (i, 0))

    out_cols = [(QN_PAD_COLS, jnp.bfloat16), (KV_COLS, jnp.float32), (KV_COLS, jnp.bfloat16),
                (WIN_COLS, jnp.float32), (WIN_COLS, jnp.bfloat16), (Q_SB_COLS, jnp.bfloat16),
                (LANES, jnp.float32), (n_mg, jnp.bfloat16)]
    return pl.pallas_call(
        _proj_body,
        grid=(m // tm,),
        in_specs=[rows(d), pl.BlockSpec((1, d), lambda i: (0, 0)), pl.BlockSpec((d, n), lambda i: (0, 0))],
        out_specs=[rows(c) for c, _ in out_cols],
        out_shape=[jax.ShapeDtypeStruct((m, c), t) for c, t in out_cols],
        compiler_params=_params("parallel"),
        name="proj",
    )(x2, g, w)


def _compress_weights(cmp_w1, cmp_w2, cmp_pe):
    w1 = cmp_w1.reshape(2, 2, CMP_STRIDE, HEAD_DIM, CMP_HIDDEN)
    pair = jnp.concatenate([w1[:, 0], w1[:, 1]], axis=-1)
    gsel = jnp.eye(NSA_GROUPS, dtype=cmp_w1.dtype)
    w1s = (gsel[None, :, None, :, None, None] * pair[:, None, :, None, :, :]).reshape(
        2, NSA_GROUPS, CMP_STRIDE, LANES, 2 * CMP_HIDDEN)
    w2s = (cmp_w2[:, None, :, None, :] * gsel[None, :, None, :, None]).reshape(2, NSA_GROUPS, CMP_HIDDEN, LANES)
    pe = jnp.pad(cmp_pe.reshape(2, 1, CMP_LEN * HEAD_DIM), ((0, 0), (0, 7), (0, 0)))
    w1f = cmp_w1.reshape(2, CMP_LEN * HEAD_DIM, CMP_HIDDEN)
    return (w1s.astype(jnp.bfloat16), w2s.astype(jnp.bfloat16), pe.astype(jnp.bfloat16), w1f.astype(jnp.bfloat16))


def _compress_core(src_refs, n_chunk, w1s_ref, w2s_ref, pe_ref, w1f_ref):
    outs = []
    for z in range(2):
        const = _dot(pe_ref[z], w1f_ref[z])[0:1]
        acc = [jnp.zeros((n_chunk, 2 * CMP_HIDDEN), jnp.float32) for _ in range(NSA_GROUPS)]
        for rho in range(CMP_STRIDE):
            xr = src_refs[z][pl.ds(rho, n_chunk, stride=CMP_STRIDE), :].astype(jnp.bfloat16)
            for g in range(NSA_GROUPS):
                acc[g] = acc[g] + _dot(xr, w1s_ref[z, g, rho])
        out = jnp.zeros((n_chunk, LANES), jnp.float32)
        for g in range(NSA_GROUPS):
            nxt = pltpu.roll(acc[g][:, CMP_HIDDEN:], n_chunk - 1, 0)
            hid = _gelu_tanh(acc[g][:, :CMP_HIDDEN] + nxt + const)
            out = out + _dot(hid.astype(jnp.bfloat16), w2s_ref[z, g])
        outs.append(out)
    return outs


def _compress_body(k_ref, v_ref, w1s_ref, w2s_ref, pe_ref, w1f_ref, kc_ref, vc_ref):
    n_chunk = kc_ref.shape[0]
    kc, vc = _compress_core((k_ref, v_ref), n_chunk, w1s_ref, w2s_ref, pe_ref, w1f_ref)
    kc_ref[...] = kc.astype(kc_ref.dtype)
    vc_ref[...] = vc.astype(vc_ref.dtype)


def _full(a):
    nd = a.ndim
    return pl.BlockSpec(a.shape, lambda *_: (0,) * nd)


def _compress_prompt(kv32, cw, batch, seq):
    n_chunk = seq // CMP_STRIDE
    kv3 = kv32.reshape(batch, seq, KV_COLS)
    out = jax.ShapeDtypeStruct((batch, n_chunk, LANES), jnp.bfloat16)
    return pl.pallas_call(
        _compress_body,
        grid=(batch,),
        in_specs=[pl.BlockSpec((None, seq, LANES), lambda b: (b, 0, BLK_CMP_K)),
                  pl.BlockSpec((None, seq, LANES), lambda b: (b, 0, BLK_CMP_V))] + [_full(a) for a in cw],
        out_specs=[pl.BlockSpec((None, n_chunk, LANES), lambda b: (b, 0, 0))] * 2,
        out_shape=[out, out],
        compiler_params=_params("parallel"),
        name="compress",
    )(kv3, kv3, *cw)


def _overlap_consts(n_chunk, n_sel_lanes):
    cs = np.arange(n_chunk)[:, None] * CMP_STRIDE
    ss = np.arange(n_sel_lanes)[None, :] * SEL_LEN
    m = ((cs < ss + SEL_LEN) & (cs + CMP_LEN > ss)).astype(np.float32)
    if n_sel_lanes == LANES:
        return jnp.asarray(m[None], jnp.bfloat16)
    z = np.zeros_like(m)
    return jnp.asarray(np.stack([np.concatenate([m, z], 1), np.concatenate([z, m], 1)]), jnp.bfloat16)


def _upper_tri():
    j = np.arange(LANES)[:, None]
    s = np.arange(LANES)[None, :]
    return jnp.asarray((j > s).astype(np.float32), jnp.bfloat16)


def _flash_step(s, valid, v_tile, m_ref, l_ref, acc_ref, idx):
    s = jnp.where(valid, s, NEG_INF)
    m_old = m_ref[idx]
    m_new = jnp.maximum(m_old, jnp.max(s, axis=-1, keepdims=True))
    alpha = jnp.exp(m_old - m_new)
    p = jnp.where(valid, jnp.exp(s - m_new), 0.0)
    l_ref[idx] = alpha * l_ref[idx] + jnp.sum(p, axis=-1, keepdims=True)
    acc_ref[idx] = alpha * acc_ref[idx] + _dot(p.astype(v_tile.dtype), v_tile)
    m_ref[idx] = m_new


def _flash_init(m_ref, l_ref, acc_ref):
    m_ref[...] = jnp.full(m_ref.shape, NEG_INF, jnp.float32)
    l_ref[...] = jnp.zeros(l_ref.shape, jnp.float32)
    acc_ref[...] = jnp.zeros(acc_ref.shape, jnp.float32)


def _nsa_body(q_ref, kc_ref, vc_ref, ks_ref, vs_ref, kw_ref, vw_ref, ng_ref, ov_ref, o_ref,
              mem_ref, out_ref, m_ref, l_ref, acc_ref, *, tq, tk, n_cmp):
    i = pl.program_id(1)
    n_chunk = kc_ref.shape[0]
    lane = _iota((tq, LANES), 1)
    row = _iota((tq, 1), 0)
    q_pos = i * tq + row

    ncol = _iota((tq, n_chunk), 1)
    d_cmp = q_pos - (ncol * CMP_STRIDE + (CMP_LEN - 1))
    valid_cmp = (d_cmp >= 0) & (ncol < n_cmp)
    dist_cmp = d_cmp.astype(jnp.float32)
    kc = kc_ref[...]
    vc = vc_ref[...]
    ps = jnp.zeros((tq, LANES), jnp.float32)
    for g in range(NSA_GROUPS):
        psum = jnp.zeros((tq, n_chunk), jnp.float32)
        for r in range(NSA_REP):
            h = g * NSA_REP + r
            s = _dot_nt(q_ref[:, h * LANES:(h + 1) * LANES], kc) - _SLOPES[h] * dist_cmp
            s = jnp.where(valid_cmp, s, NEG_INF)
            e = jnp.where(valid_cmp, jnp.exp(s - jnp.max(s, axis=-1, keepdims=True)), 0.0)
            p = e / jnp.maximum(jnp.sum(e, axis=-1, keepdims=True), 1e-30)
            psum = psum + p
            out_ref[h] = ng_ref[:, 3 * h:3 * h + 1] * _dot(p.astype(vc.dtype), vc)
        ps = ps + _dot_split(psum, ov_ref[g])

    blk = lane & (SEL_LEN - 1)
    cur = q_pos >> 6
    valid_blk = blk <= cur
    forced = (blk == 0) | (blk == cur) | (blk == cur - 1)
    score = jnp.where(valid_blk, ps + jnp.where(forced, FORCE_BONUS, 0.0), NEG_INF)
    low_half = lane < SEL_LEN
    rank = jnp.zeros((tq, LANES), jnp.float32)
    for c in range(SEL_LEN):
        col = jnp.where(low_half, score[:, c:c + 1], score[:, SEL_LEN + c:SEL_LEN + c + 1])
        beats = (col > score) | ((col == score) & (c < blk))
        rank = rank + jnp.where(beats, 1.0, 0.0)
    top_k = min(SEL_TOPK, ks_ref.shape[0] // SEL_LEN)
    mem_ref[...] = jnp.where((rank < top_k) & valid_blk, 1.0, 0.0).astype(mem_ref.dtype)

    rel = row - _iota((tq, tk), 1)
    erow = _iota((LANES, tk), 0)
    ecol_hi = jnp.where(_iota((LANES, tk), 1) >= SEL_LEN, 1, 0)

    def run(k_ref, v_ref, g, kt_lo, kt_hi, use_sel):
        _flash_init(m_ref, l_ref, acc_ref)

        def step(t, carry):
            kt = kt_hi - t
            off = pl.multiple_of(kt * tk, tk)
            k_tile = k_ref[pl.ds(off, tk), :]
            v_tile = v_ref[pl.ds(off, tk), :]
            d = (i * tq - kt * tk) + rel
            if use_sel:
                expand = jnp.where(erow == g * SEL_LEN + kt * (tk // SEL_LEN) + ecol_hi, 1.0, 0.0)
                picked = _dot(mem_ref[...], expand.astype(mem_ref.dtype))
                valid = (picked > 0.5) & (d >= 0)
            else:
                valid = (d >= 0) & (d < WINDOW)
            dist = d.astype(jnp.float32)
            for r in range(NSA_REP):
                h = g * NSA_REP + r
                s = _dot_nt(q_ref[:, h * LANES:(h + 1) * LANES], k_tile) - _SLOPES[h] * dist
                _flash_step(s, valid, v_tile, m_ref, l_ref, acc_ref, r)
            return carry

        lax.fori_loop(0, kt_hi - kt_lo + 1, step, 0)

    kt_diag = (i * tq + tq - 1) // tk
    kt_win = jnp.maximum(i * tq - (WINDOW - 1), 0) // tk
    for g in range(NSA_GROUPS):
        for branch, (k_ref, v_ref, lo, use_sel) in enumerate(
                [(ks_ref, vs_ref, 0, True), (kw_ref, vw_ref, kt_win, False)]):
            run(k_ref, v_ref, g, lo, kt_diag, use_sel)
            for r in range(NSA_REP):
                h = g * NSA_REP + r
                o = acc_ref[r] / jnp.maximum(l_ref[r], 1e-30)
                out_ref[h] = out_ref[h] + ng_ref[:, 3 * h + 1 + branch:3 * h + 2 + branch] * o
    for h in range(NSA_HEADS):
        keep = (lane >= SEL_LEN) == (h >= NSA_REP)
        o_ref[:, h * LANES:(h + 1) * LANES] = jnp.where(keep, out_ref[h], 0.0).astype(o_ref.dtype)


def _nsa_prompt(qn, kvb, winb, kc, vc, ng, batch, seq):
    tq = tk = 128
    n_chunk = seq // CMP_STRIDE
    n_cmp = (seq - CMP_LEN) // CMP_STRIDE + 1
    assert seq % tq == 0 and seq // SEL_LEN <= SEL_LEN
    ov = _overlap_consts(n_chunk, SEL_LEN)
    kv3 = kvb.reshape(batch, seq, KV_COLS)
    win3 = winb.reshape(batch, seq, WIN_COLS)

    def seq_blk(c):
        return pl.BlockSpec((None, seq, LANES), lambda b, i: (b, 0, c))

    def cmp_blk():
        return pl.BlockSpec((None, n_chunk, LANES), lambda b, i: (b, 0, 0))

    nq = seq // tq
    return pl.pallas_call(
        functools.partial(_nsa_body, tq=tq, tk=tk, n_cmp=n_cmp),
        grid=(batch, nq),
        in_specs=[pl.BlockSpec((tq, QN_PAD_COLS), lambda b, i: (b * nq + i, 0)),
                  cmp_blk(), cmp_blk(),
                  seq_blk(BLK_SEL_K), seq_blk(BLK_SEL_V),
                  pl.BlockSpec((None, seq, LANES), lambda b, i: (b, 0, 0)),
                  pl.BlockSpec((None, seq, LANES), lambda b, i: (b, 0, 1)),
                  pl.BlockSpec((tq, LANES), lambda b, i: (b * nq + i, 0)),
                  _full(ov)],
        out_specs=pl.BlockSpec((tq, QN_PAD_COLS), lambda b, i: (b * nq + i, 0)),
        out_shape=jax.ShapeDtypeStruct((batch * seq, QN_PAD_COLS), jnp.bfloat16),
        scratch_shapes=[pltpu.VMEM((tq, LANES), jnp.bfloat16),
                        pltpu.VMEM((NSA_HEADS, tq, LANES), jnp.float32),
                        pltpu.VMEM((NSA_REP, tq, 1), jnp.float32),
                        pltpu.VMEM((NSA_REP, tq, 1), jnp.float32),
                        pltpu.VMEM((NSA_REP, tq, LANES), jnp.float32)],
        compiler_params=_params("parallel", "parallel"),
        name="nsa",
    )(qn, kc, vc, kv3, kv3, win3, win3, ng, ov)


def _softplus(z):
    return jnp.maximum(z, 0.0) + jnp.log1p(jnp.exp(-jnp.abs(z)))


def _sb_body(q_ref, k_ref, v_ref, u_ref, o_ref, *, tq, tk):
    i = pl.program_id(2)
    lane = _iota((tq, LANES), 1)
    rel = _iota((tq, 1), 0) - _iota((tq, tk), 1)
    q2 = q_ref[...].astype(jnp.float32)
    u = u_ref[...]
    res = []
    for e in range(2):
        qe = jnp.where((lane >= HEAD_DIM) == (e == 1), q2, 0.0).astype(jnp.bfloat16)

        def step(t, carry):
            rest_later, acc = carry
            kt = i - t
            off = pl.multiple_of(kt * tk, tk)
            z = _dot_nt(qe, k_ref[pl.ds(off, tk), :])
            past = (t * tk + rel) > 0
            sp = _softplus(z)
            log_keep = jnp.where(past, -sp, 0.0)
            rest = _dot_split(log_keep, u)
            a = jnp.where(past, jnp.exp((z - sp) + rest + rest_later), 0.0)
            acc = acc + _dot(a.astype(v_ref.dtype), v_ref[pl.ds(off, tk), :])
            return rest_later + (rest[:, 0:1] + log_keep[:, 0:1]), acc

        _, acc = lax.fori_loop(0, i + 1, step, (jnp.zeros((tq, 1), jnp.float32), jnp.zeros((tq, LANES), jnp.float32)))
        res.append(acc)
    o_ref[...] = jnp.where(lane < HEAD_DIM, res[0], res[1]).astype(o_ref.dtype)


def _sb_prompt(qs, kvb, batch, seq):
    tq = tk = 128
    assert seq % tq == 0
    kv3 = kvb.reshape(batch, seq, KV_COLS)
    nq = seq // tq
    n_pair = SB_HEADS // 2
    u = _upper_tri()
    return pl.pallas_call(
        functools.partial(_sb_body, tq=tq, tk=tk),
        grid=(batch, n_pair, nq),
        in_specs=[pl.BlockSpec((tq, LANES), lambda b, p, i: (b * nq + i, p)),
                  pl.BlockSpec((None, seq, LANES), lambda b, p, i: (b, 0, BLK_SB_K + p)),
                  pl.BlockSpec((None, seq, LANES), lambda b, p, i: (b, 0, BLK_SB_V + p)),
                  _full(u)],
        out_specs=pl.BlockSpec((tq, LANES), lambda b, p, i: (b * nq + i, p)),
        out_shape=jax.ShapeDtypeStruct((batch * seq, Q_SB_COLS), jnp.bfloat16),
        compiler_params=_params("parallel", "parallel", "parallel"),
        name="sb",
    )(qs, kv3, kv3, u)


def _sample_a_body(pt_ref, pool_ref, q_ref, w1s_ref, w2s_ref, pe_ref, w1f_ref, ov_ref, ocmp_ref, mem_ref, past_ref,
                   *, page, n_pages):
    del pt_ref
    j = pl.program_id(1)
    rows = pl.ds(pl.multiple_of(j * page, page), page)
    past_ref[0, rows, :] = pool_ref[:, BLK_CMP_K * LANES:(BLK_CMP_K + 1) * LANES]
    past_ref[1, rows, :] = pool_ref[:, BLK_CMP_V * LANES:(BLK_CMP_V + 1) * LANES]

    @pl.when(j == n_pages - 1)
    def _():
        past_len = n_pages * page
        n_chunk = past_len // CMP_STRIDE
        n_cmp = (past_len + 1 - CMP_LEN) // CMP_STRIDE + 1
        n_past_blk = past_len // SEL_LEN
        kc, vc = _compress_core((past_ref.at[0], past_ref.at[1]), n_chunk, w1s_ref, w2s_ref, pe_ref, w1f_ref)
        kc = kc.astype(jnp.bfloat16)
        vc = vc.astype(jnp.bfloat16)
        q8 = q_ref[...]
        ncol = _iota((NSA_HEADS, n_chunk), 1)
        d = past_len - (ncol * CMP_STRIDE + (CMP_LEN - 1))
        valid = (d >= 0) & (ncol < n_cmp)
        hrow = _iota((NSA_HEADS, 1), 0)
        s = _dot_nt(q8, kc) - _slope_col() * d.astype(jnp.float32)
        s = jnp.where(valid, s, NEG_INF)
        e = jnp.where(valid, jnp.exp(s - jnp.max(s, axis=-1, keepdims=True)), 0.0)
        p = e / jnp.maximum(jnp.sum(e, axis=-1, keepdims=True), 1e-30)
        ocmp_ref[...] = _dot(p.astype(vc.dtype), vc)
        psum = jnp.zeros_like(p)
        for g in range(NSA_GROUPS):
            pg = jnp.sum(p[g * NSA_REP:(g + 1) * NSA_REP], axis=0, keepdims=True)
            psum = jnp.where((hrow >= g * NSA_REP) & (hrow < (g + 1) * NSA_REP), pg, psum)
        ps = _dot_split(psum, ov_ref[0])
        blk = _iota((NSA_HEADS, LANES), 1)
        forced = (blk == 0) | (blk == n_past_blk - 1)
        score = jnp.where(blk < n_past_blk, ps + jnp.where(forced, FORCE_BONUS, 0.0), NEG_INF)
        eye = _iota((LANES, LANES), 0) == _iota((LANES, LANES), 1)
        before = _iota((LANES, LANES), 0) < _iota((LANES, LANES), 1)
        for g in range(NSA_GROUPS):
            srow = score[g * NSA_REP:g * NSA_REP + 1, :]
            scol = jnp.sum(jnp.where(eye, srow, 0.0), axis=1, keepdims=True)
            beats = (scol > srow) | ((scol == srow) & before)
            rank = jnp.sum(jnp.where(beats, 1.0, 0.0), axis=0, keepdims=True)
            top_k = min(SEL_TOPK, n_past_blk + 1)
            member = jnp.where((rank < top_k - 1) & (blk[0:1] < n_past_blk), 1.0, 0.0)
            mem_ref[g * NSA_REP:(g + 1) * NSA_REP, :] = jnp.broadcast_to(member, (NSA_REP, LANES))


def _sample_a(pool3, pt_flat, q8, cw, dec, n_pages, page):
    past_len = n_pages * page
    n_chunk = past_len // CMP_STRIDE
    assert past_len // SEL_LEN <= LANES and past_len % SEL_LEN == 0
    ov = _overlap_consts(n_chunk, LANES)
    o8 = jax.ShapeDtypeStruct((dec, NSA_HEADS, LANES), jnp.float32)

    def const(a):
        nd = a.ndim
        return pl.BlockSpec(a.shape, lambda b, j, pt: (0,) * nd)

    grid_spec = pltpu.PrefetchScalarGridSpec(
        num_scalar_prefetch=1,
        grid=(dec, n_pages),
        in_specs=[pl.BlockSpec((None, page, 2 * LANES), lambda b, j, pt: (pt[b * n_pages + j], 0, 0)),
                  pl.BlockSpec((None, NSA_HEADS, LANES), lambda b, j, pt: (b, 0, 0))]
                 + [const(a) for a in cw] + [const(ov)],
        out_specs=[pl.BlockSpec((None, NSA_HEADS, LANES), lambda b, j, pt: (b, 0, 0))] * 2,
        scratch_shapes=[pltpu.VMEM((2, past_len, LANES), jnp.float32)],
    )
    return pl.pallas_call(
        functools.partial(_sample_a_body, page=page, n_pages=n_pages),
        grid_spec=grid_spec,
        out_shape=[o8, o8],
        compiler_params=_params("parallel", "arbitrary"),
        name="sample_a",
    )(pt_flat, pool3, q8, *cw, ov)


def _sample_b_body(pt_ref, pool_ref, qn_ref, qs_ref, kvn_ref, winn_ref, cw_ref, mem_ref, ocmp_ref, gate_ref, u_ref,
                   oa_ref, ob_ref, wout_ref, m_ref, l_ref, acc_ref, rest_ref, sbacc_ref, *, page, n_pages, wbuf):
    del pt_ref
    jj = pl.program_id(1)
    j = n_pages - 1 - jj
    past_len = n_pages * page
    q8 = qn_ref[...]
    hrow = _iota((NSA_HEADS, 1), 0)
    slope = _slope_col()
    lane8 = _iota((NSA_HEADS, LANES), 1)
    own_half = (lane8 >= HEAD_DIM) == (hrow >= NSA_REP)

    def new_row(ref, c):
        return ref[:, c * LANES:(c + 1) * LANES].astype(jnp.bfloat16).astype(jnp.float32)

    @pl.when(jj == 0)
    def _():
        qf = q8.astype(jnp.float32)
        m_ref[0] = jnp.sum(qf * new_row(kvn_ref, BLK_SEL_K), axis=-1, keepdims=True)
        l_ref[0] = jnp.ones((NSA_HEADS, 1), jnp.float32)
        acc_ref[0] = jnp.broadcast_to(new_row(kvn_ref, BLK_SEL_V), (NSA_HEADS, LANES))
        rest_ref[...] = jnp.zeros(rest_ref.shape, jnp.float32)
        sbacc_ref[...] = jnp.zeros(sbacc_ref.shape, jnp.float32)

    kcol = _iota((NSA_HEADS, page), 1)
    d = past_len - (j * page + kcol)
    expand = jnp.where(_iota((LANES, page), 0) == j * (page // SEL_LEN) + (_iota((LANES, page), 1) >> 6), 1.0, 0.0)
    picked = _dot(mem_ref[...].astype(jnp.bfloat16), expand.astype(jnp.bfloat16))
    k_sel = pool_ref[:, BLK_SEL_K * LANES:(BLK_SEL_K + 1) * LANES].astype(jnp.bfloat16)
    v_sel = pool_ref[:, BLK_SEL_V * LANES:(BLK_SEL_V + 1) * LANES].astype(jnp.bfloat16)
    s = _dot_nt(q8, k_sel) - slope * d.astype(jnp.float32)
    _flash_step(s, picked > 0.5, v_sel, m_ref, l_ref, acc_ref, 0)

    qs = qs_ref[...]
    lane_sb = _iota((SB_HEADS, Q_SB_COLS), 1)
    head_sb = _iota((SB_HEADS, Q_SB_COLS), 0)
    own_sb = (lane_sb >> 6) == head_sb
    qbd = jnp.where(own_sb, qs.astype(jnp.float32), 0.0).astype(jnp.bfloat16)
    k_sb = pool_ref[:, BLK_SB_K * LANES:BLK_SB_V * LANES].astype(jnp.bfloat16)
    v_sb = pool_ref[:, BLK_SB_V * LANES:KV_COLS].astype(jnp.bfloat16)
    z = _dot_nt(qbd, k_sb)
    sp = _softplus(z)
    log_keep = -sp
    rest = _dot_split(log_keep, u_ref[...])
    a = jnp.exp((z - sp) + rest + rest_ref[...])
    sbacc_ref[...] = sbacc_ref[...] + _dot(a.astype(jnp.bfloat16), v_sb)
    rest_ref[...] = rest_ref[...] + (rest[:, 0:1] + log_keep[:, 0:1])

    @pl.when(jj == n_pages - 1)
    def _():
        o_sel = acc_ref[0] / jnp.maximum(l_ref[0], 1e-30)
        cw = cw_ref[...]
        k_w = cw[:, 0:LANES].astype(jnp.bfloat16)
        v_w = cw[:, LANES:2 * LANES].astype(jnp.bfloat16)
        wcol = _iota((NSA_HEADS, wbuf), 1)
        dw = wbuf - wcol
        valid_w = (dw < WINDOW) & (past_len - dw >= 0)
        s_w = jnp.where(valid_w, _dot_nt(q8, k_w) - slope * dw.astype(jnp.float32), NEG_INF)
        s_n = jnp.sum(q8.astype(jnp.float32) * new_row(winn_ref, 0), axis=-1, keepdims=True)
        m_w = jnp.maximum(jnp.max(s_w, axis=-1, keepdims=True), s_n)
        p_w = jnp.where(valid_w, jnp.exp(s_w - m_w), 0.0)
        p_n = jnp.exp(s_n - m_w)
        l_w = jnp.sum(p_w, axis=-1, keepdims=True) + p_n
        o_win = (_dot(p_w.astype(jnp.bfloat16), v_w)
                 + p_n.astype(jnp.bfloat16).astype(jnp.float32) * new_row(winn_ref, 1)) / jnp.maximum(l_w, 1e-30)
        gate = gate_ref[...]
        o_a = gate[:, 0:1] * ocmp_ref[...] + gate[:, 1:2] * o_sel + gate[:, 2:3] * o_win
        oa_ref[...] = jnp.where(own_half, o_a, 0.0)
        ob_ref[...] = jnp.sum(jnp.where(own_sb, sbacc_ref[...], 0.0), axis=0, keepdims=True)
        wrow = _iota((wbuf, 1), 0)
        wout_ref[...] = jnp.where(wrow == wbuf - 1, winn_ref[...], pltpu.roll(cw, wbuf - 1, 0))


def _sample_b(pool3, pt_flat, q8, qs, kv32, win32, cache_win2, member, ocmp, gate, dec, n_pages, page):
    wbuf = cache_win2.shape[1]
    u = _upper_tri()
    assert page == LANES

    def per_seq(*shape):
        nd = len(shape)
        return pl.BlockSpec((None,) + shape, lambda b, jj, pt: (b,) + (0,) * nd)

    grid_spec = pltpu.PrefetchScalarGridSpec(
        num_scalar_prefetch=1,
        grid=(dec, n_pages),
        in_specs=[pl.BlockSpec((None, page, KV_COLS), lambda b, jj, pt: (pt[b * n_pages + n_pages - 1 - jj], 0, 0)),
                  per_seq(NSA_HEADS, LANES), per_seq(1, Q_SB_COLS), per_seq(1, KV_COLS), per_seq(1, WIN_COLS),
                  per_seq(wbuf, WIN_COLS), per_seq(NSA_HEADS, LANES), per_seq(NSA_HEADS, LANES),
                  per_seq(NSA_HEADS, 3),
                  pl.BlockSpec(u.shape, lambda b, jj, pt: (0, 0))],
        out_specs=[per_seq(NSA_HEADS, LANES), per_seq(1, Q_SB_COLS), per_seq(wbuf, WIN_COLS)],
        scratch_shapes=[pltpu.VMEM((1, NSA_HEADS, 1), jnp.float32), pltpu.VMEM((1, NSA_HEADS, 1), jnp.float32),
                        pltpu.VMEM((1, NSA_HEADS, LANES), jnp.float32), pltpu.VMEM((SB_HEADS, 1), jnp.float32),
                        pltpu.VMEM((SB_HEADS, Q_SB_COLS), jnp.float32)],
    )
    return pl.pallas_call(
        functools.partial(_sample_b_body, page=page, n_pages=n_pages, wbuf=wbuf),
        grid_spec=grid_spec,
        out_shape=[jax.ShapeDtypeStruct((dec, NSA_HEADS, LANES), jnp.float32),
                   jax.ShapeDtypeStruct((dec, 1, Q_SB_COLS), jnp.float32),
                   jax.ShapeDtypeStruct((dec, wbuf, WIN_COLS), jnp.float32)],
        compiler_params=_params("parallel", "arbitrary"),
        name="sample_b",
    )(pt_flat, pool3, q8, qs, kv32, win32, cache_win2, member, ocmp, gate, u)


def _post_body(x_ref, oa_ref, ob_ref, mg_ref, wa_ref, wb_ref, wo_ref, g_ref, y_ref):
    d = x_ref.shape[1]
    u_a = _dot(oa_ref[...].astype(jnp.bfloat16), wa_ref[...])
    u_b = _dot(ob_ref[...].astype(jnp.bfloat16), wb_ref[...])
    mixed = mg_ref[:, 0:d].astype(jnp.float32) * u_a + mg_ref[:, d:2 * d].astype(jnp.float32) * u_b
    y_ref[...] = x_ref[...] + _rms(_dot(mixed.astype(jnp.bfloat16), wo_ref[...]), g_ref[...])


def _post(x2, oa, ob, mg, wa, wb, wo, g):
    m, d = x2.shape
    tm = min(256, m)

    def rows(c):
        return pl.BlockSpec((tm, c), lambda i: (i, 0))

    return pl.pallas_call(
        _post_body,
        grid=(m // tm,),
        in_specs=[rows(d), rows(oa.shape[1]), rows(ob.shape[1]), rows(mg.shape[1]),
                  _full(wa), _full(wb), _full(wo), _full(g)],
        out_specs=rows(d),
        out_shape=jax.ShapeDtypeStruct((m, d), jnp.float32),
        compiler_params=_params("parallel"),
        name="post",
    )(x2, oa, ob, mg, wa, wb, wo, g)


def _ffn_body(x_ref, prev_ref, g2_ref, g3_ref, wup_ref, cw_ref, cb_ref, wdn_ref, y_ref, st_ref, carry_ref,
              *, tm, d_ff, fc, per_row):
    t = pl.program_id(1)
    x = x_ref[...]
    h = _rms(x, g2_ref[...]).astype(jnp.bfloat16)
    if not per_row:
        @pl.when(t == 0)
        def _():
            carry_ref[...] = prev_ref[...]
    row = _iota((tm, 1), 0)
    acc = jnp.zeros((tm, x.shape[1]), jnp.float32)
    for c0 in range(0, d_ff, fc):
        halves = []
        for base in (c0, d_ff + c0):
            cols = slice(base, base + fc)
            u = _dot(h, wup_ref[:, cols])
            if per_row:
                u1 = prev_ref[1, :, cols]
                u2 = prev_ref[0, :, cols]
                st_ref[0, :, cols] = u1
                st_ref[1, :, cols] = u
            else:
                p1 = carry_ref[1:2, cols]
                p2 = carry_ref[0:1, cols]
                u1 = jnp.where(row < 1, p1, pltpu.roll(u, 1, 0))
                u2 = jnp.where(row < 1, p2, jnp.where(row < 2, p1, pltpu.roll(u, 2, 0)))
                carry_ref[:, cols] = u[tm - 2:tm, :]
                st_ref[:, cols] = u[tm - 2:tm, :]
            halves.append(cb_ref[:, cols] + u2 * cw_ref[0:1, cols] + u1 * cw_ref[1:2, cols] + u * cw_ref[2:3, cols])
        act = (_gelu_tanh(halves[0]) * halves[1]).astype(jnp.bfloat16)
        acc = acc + _dot(act, wdn_ref[c0:c0 + fc, :])
    y_ref[...] = x + _rms(acc, g3_ref[...])


def _ffn(x2, prev, g2, g3, wup, cw, cb, wdn, batch, seq, per_row):
    m, d = x2.shape
    d_ff = wdn.shape[0]
    fc = 256
    assert d_ff % fc == 0
    if per_row:
        tm, nb, nt = m, 1, 1
        prev_spec = pl.BlockSpec((2, tm, 2 * d_ff), lambda b, t: (0, 0, 0))
        st_spec = pl.BlockSpec((2, tm, 2 * d_ff), lambda b, t: (0, 0, 0))
        st_shape = jax.ShapeDtypeStruct((2, m, 2 * d_ff), jnp.float32)
    else:
        tm = min(256, seq)
        nb, nt = batch, seq // tm
        prev_spec = pl.BlockSpec((None, 2, 2 * d_ff), lambda b, t: (b, 0, 0))
        st_spec = pl.BlockSpec((None, 2, 2 * d_ff), lambda b, t: (b, 0, 0))
        st_shape = jax.ShapeDtypeStruct((batch, 2, 2 * d_ff), jnp.float32)
    return pl.pallas_call(
        functools.partial(_ffn_body, tm=tm, d_ff=d_ff, fc=fc, per_row=per_row),
        grid=(nb, nt),
        in_specs=[pl.BlockSpec((tm, d), lambda b, t: (b * nt + t, 0)), prev_spec,
                  _full(g2), _full(g3), _full(wup), _full(cw), _full(cb), _full(wdn)],
        out_specs=[pl.BlockSpec((tm, d), lambda b, t: (b * nt + t, 0)), st_spec],
        out_shape=[jax.ShapeDtypeStruct((m, d), jnp.float32), st_shape],
        scratch_shapes=[pltpu.VMEM((2, 2 * d_ff), jnp.float32)],
        compiler_params=_params("parallel", "arbitrary"),
        name="ffn",
    )(x2, prev, g2, g3, wup, cw, cb, wdn)


def _branch_a_weights(w_branch_a):
    d = w_branch_a.shape[1]
    w = w_branch_a.reshape(NSA_GROUPS, NSA_REP, 1, HEAD_DIM, d)
    half = jnp.eye(NSA_GROUPS, dtype=w.dtype)
    return (w * half[:, None, :, None, None]).reshape(QN_PAD_COLS, d).astype(jnp.bfloat16)


def kernel(x_prompt, x_sample, cache_kv, cache_win, state_conv, page_table, norm_g, w_in, cmp_w1, cmp_w2, cmp_pe,
           w_branch_a, w_branch_b, w_out, ffn_w_up, ffn_conv_w, ffn_conv_b, ffn_w_down):
    depth = norm_g.shape[0]
    assert depth == 1, "one layer per call: outputs of deeper stacks are not wired"
    batch, seq, d_model = x_prompt.shape
    dec, dec_seq, _ = x_sample.shape
    assert dec_seq == 1
    n_pool, page = cache_kv.shape[1], cache_kv.shape[2]
    n_pages = page_table.shape[1]
    wbuf = cache_win.shape[2]
    d_ff = ffn_w_down.shape[1]
    l = 0

    g = norm_g[l].reshape(4, 1, d_model)
    wp = _proj_weights(w_in[l], d_model)
    cw = _compress_weights(cmp_w1[l], cmp_w2[l], cmp_pe[l])
    wa = _branch_a_weights(w_branch_a[l])
    wb = w_branch_b[l].astype(jnp.bfloat16)
    wo = w_out[l].astype(jnp.bfloat16)
    wup = ffn_w_up[l].astype(jnp.bfloat16)
    wdn = ffn_w_down[l].astype(jnp.bfloat16)
    conv_w = ffn_conv_w[l]
    conv_b = ffn_conv_b[l].reshape(1, 2 * d_ff)

    xp = x_prompt.reshape(batch * seq, d_model)
    qn, kv32, kvb, win32, winb, qs, ng, mg = _proj(xp, g[0], wp)
    kc, vc = _compress_prompt(kv32, cw, batch, seq)
    oa = _nsa_prompt(qn, kvb, winb, kc, vc, ng, batch, seq)
    ob = _sb_prompt(qs, kvb, batch, seq)
    x1 = _post(xp, oa, ob, mg, wa, wb, wo, g[1])
    conv0 = jnp.zeros((batch, CONV_W - 1, 2 * d_ff), jnp.float32)
    yp, conv_p = _ffn(x1, conv0, g[2], g[3], wup, conv_w, conv_b, wdn, batch, seq, per_row=False)
    kv_p = kv32.reshape(1, batch, seq, KV_ROWS, HEAD_DIM)
    win_seq = win32.reshape(batch, seq, WIN_ROWS, HEAD_DIM)
    win_p = jnp.pad(win_seq, ((0, 0), (wbuf, 0), (0, 0), (0, 0)))[:, -wbuf:][None]

    xs = x_sample.reshape(dec, d_model)
    qn_s, kv32_s, _, win32_s, _, qs_s, ng_s, mg_s = _proj(xs, g[0], wp)
    pool3 = cache_kv[l].reshape(n_pool, page, KV_COLS)
    pt_flat = page_table.reshape(dec * n_pages)
    q8 = qn_s.reshape(dec, NSA_HEADS, LANES)
    ocmp, member = _sample_a(pool3, pt_flat, q8, cw, dec, n_pages, page)
    gate = ng_s[:, :NSA_GATE_COLS].reshape(dec, NSA_HEADS, 3)
    oa_s, ob_s, win_s = _sample_b(pool3, pt_flat, q8, qs_s.reshape(dec, 1, Q_SB_COLS),
                                  kv32_s.reshape(dec, 1, KV_COLS), win32_s.reshape(dec, 1, WIN_COLS),
                                  cache_win[l].reshape(dec, wbuf, WIN_COLS), member, ocmp, gate, dec, n_pages, page)
    x1_s = _post(xs, oa_s.reshape(dec, QN_PAD_COLS), ob_s.reshape(dec, Q_SB_COLS), mg_s, wa, wb, wo, g[1])
    prev_s = jnp.swapaxes(state_conv[l], 0, 1)
    ys, conv_s = _ffn(x1_s, prev_s, g[2], g[3], wup, conv_w, conv_b, wdn, dec, 1, per_row=True)

    return (yp.reshape(batch, seq, d_model), ys.reshape(dec, 1, d_model), kv_p,
            kv32_s.reshape(1, dec, 1, KV_ROWS, HEAD_DIM), win_p,
            win_s.reshape(1, dec, wbuf, WIN_ROWS, HEAD_DIM), conv_p[None],
            jnp.swapaxes(conv_s, 0, 1)[None])
```

```python
import functools
import math

import numpy as np
import jax
import jax.numpy as jnp
from jax import lax
from jax.experimental import pallas as pl
from jax.experimental.pallas import tpu as pltpu

HEAD_DIM = 64
NSA_HEADS = 8
NSA_GROUPS = 2
NSA_REP = NSA_HEADS // NSA_GROUPS
SB_HEADS = 8
CMP_LEN = 32
CMP_STRIDE = 16
CMP_HIDDEN = 128
SEL_LEN = 64
SEL_TOPK = 16
WINDOW = 512
CONV_W = 3
RMS_EPS = 1e-6
NEG_INF = -1e30
FORCE_BONUS = 1e6
ATTN_SCALE = HEAD_DIM ** -0.5
KV_ROWS = 4 * NSA_GROUPS + 2 * SB_HEADS
WIN_ROWS = 2 * NSA_GROUPS

LANES = 128
Q_NSA_COLS = NSA_HEADS * HEAD_DIM
KV_COLS = KV_ROWS * HEAD_DIM
WIN_COLS = WIN_ROWS * HEAD_DIM
Q_SB_COLS = SB_HEADS * HEAD_DIM
NSA_GATE_COLS = 3 * NSA_HEADS
QN_PAD_COLS = NSA_HEADS * LANES

BLK_CMP_K, BLK_CMP_V, BLK_SEL_K, BLK_SEL_V = 0, 1, 2, 3
BLK_SB_K = 4
BLK_SB_V = 4 + SB_HEADS // 2

VMEM_LIMIT = 56 * 1024 * 1024

_SLOPES = [float(np.power(np.float32(2.0), np.float32(-8.0 * (h + 1) / NSA_HEADS))) for h in range(NSA_HEADS)]


def _dot(a, b):
    return jnp.dot(a, b, preferred_element_type=jnp.float32)


def _dot_nt(a, b):
    return lax.dot_general(a, b, (((1,), (1,)), ((), ())), preferred_element_type=jnp.float32)


def _dot_split(x, w):
    hi = x.astype(jnp.bfloat16)
    lo = (x - hi.astype(jnp.float32)).astype(jnp.bfloat16)
    return _dot(hi, w) + _dot(lo, w)


def _rms(x, g):
    return x * lax.rsqrt(jnp.mean(x * x, axis=-1, keepdims=True) + RMS_EPS) * g


def _gelu_tanh(x):
    return 0.5 * x * (1.0 + jnp.tanh(math.sqrt(2.0 / math.pi) * (x + 0.044715 * (x * x * x))))


def _sigmoid(x):
    return 1.0 / (1.0 + jnp.exp(-x))


def _iota(shape, dim):
    return lax.broadcasted_iota(jnp.int32, shape, dim)


def _slope_col():
    hrow = _iota((NSA_HEADS, 1), 0)
    out = jnp.zeros((NSA_HEADS, 1), jnp.float32)
    for h in range(NSA_HEADS):
        out = jnp.where(hrow == h, _SLOPES[h], out)
    return out


def _params(*sem):
    return pltpu.CompilerParams(dimension_semantics=sem, vmem_limit_bytes=VMEM_LIMIT)


_SEG_QN = (0, QN_PAD_COLS)
_SEG_KV = (_SEG_QN[1], _SEG_QN[1] + KV_COLS)
_SEG_WIN = (_SEG_KV[1], _SEG_KV[1] + WIN_COLS)
_SEG_QS = (_SEG_WIN[1], _SEG_WIN[1] + Q_SB_COLS)
_SEG_NG = (_SEG_QS[1], _SEG_QS[1] + LANES)
PROJ_PAD_COLS_FIXED = _SEG_NG[1]


def _proj_weights(w_in, d_model):
    o1 = Q_NSA_COLS
    o2 = o1 + KV_COLS
    o3 = o2 + WIN_COLS
    o4 = o3 + Q_SB_COLS
    o5 = o4 + NSA_GATE_COLS
    wq = (w_in[:, :o1] * ATTN_SCALE).reshape(d_model, NSA_GROUPS, NSA_REP, 1, HEAD_DIM)
    half = (jnp.arange(NSA_GROUPS)[:, None] == jnp.arange(2)[None, :]).astype(w_in.dtype)
    wq = (wq * half[None, :, None, :, None]).reshape(d_model, QN_PAD_COLS)
    ng = jnp.pad(w_in[:, o4:o5], ((0, 0), (0, LANES - NSA_GATE_COLS)))
    w = jnp.concatenate([wq, w_in[:, o1:o3], w_in[:, o3:o4] * ATTN_SCALE, ng, w_in[:, o5:]], axis=1)
    return w.astype(jnp.bfloat16)


def _proj_body(x_ref, g_ref, w_ref, qn_ref, kv_ref, kvb_ref, win_ref, winb_ref, qs_ref, ng_ref, mg_ref):
    h = _rms(x_ref[...], g_ref[...]).astype(jnp.bfloat16)

    def seg(c0, c1, fn, step=512):
        for a in range(c0, c1, step):
            b = min(a + step, c1)
            fn(a - c0, b - c0, _dot(h, w_ref[:, a:b]))

    def put_qn(a, b, r):
        for c in range(a, b, LANES):
            qn_ref[c // LANES] = r[:, c - a:c - a + LANES].astype(qn_ref.dtype)

    def put_kv(a, b, r):
        kv_ref[:, a:b] = r
        kvb_ref[:, a:b] = r.astype(kvb_ref.dtype)

    def put_win(a, b, r):
        win_ref[:, a:b] = r
        winb_ref[:, a:b] = r.astype(winb_ref.dtype)

    def put_qs(a, b, r):
        qs_ref[:, a:b] = r.astype(qs_ref.dtype)

    def put_ng(a, b, r):
        ng_ref[:, a:b] = _sigmoid(r)

    def put_mg(a, b, r):
        mg_ref[:, a:b] = _sigmoid(r).astype(mg_ref.dtype)

    seg(*_SEG_QN, put_qn)
    seg(*_SEG_KV, put_kv)
    seg(*_SEG_WIN, put_win)
    seg(*_SEG_QS, put_qs)
    seg(*_SEG_NG, put_ng)
    seg(PROJ_PAD_COLS_FIXED, w_ref.shape[1], put_mg)


def _proj(x2, g, w):
    m, d = x2.shape
    tm = min(256, m)
    n = w.shape[1]
    n_mg = n - PROJ_PAD_COLS_FIXED

    def rows(c):
        return pl.BlockSpec((tm, c), lambda i: (i, 0))

    out_cols = [(KV_COLS, jnp.float32), (KV_COLS, jnp.bfloat16),
                (WIN_COLS, jnp.float32), (WIN_COLS, jnp.bfloat16), (Q_SB_COLS, jnp.bfloat16),
                (LANES, jnp.float32), (n_mg, jnp.bfloat16)]
    return pl.pallas_call(
        _proj_body,
        grid=(m // tm,),
        in_specs=[rows(d), pl.BlockSpec((1, d), lambda i: (0, 0)), pl.BlockSpec((d, n), lambda i: (0, 0))],
        out_specs=[pl.BlockSpec((NSA_HEADS, tm, LANES), lambda i: (0, i, 0))] + [rows(c) for c, _ in out_cols],
        out_shape=[jax.ShapeDtypeStruct((NSA_HEADS, m, LANES), jnp.bfloat16)]
                  + [jax.ShapeDtypeStruct((m, c), t) for c, t in out_cols],
        compiler_params=_params("parallel"),
        name="proj",
    )(x2, g, w)


def _compress_weights(cmp_w1, cmp_w2, cmp_pe):
    w1 = cmp_w1.reshape(2, 2, CMP_STRIDE, HEAD_DIM, CMP_HIDDEN)
    pair = jnp.concatenate([w1[:, 0], w1[:, 1]], axis=-1)
    gsel = jnp.eye(NSA_GROUPS, dtype=cmp_w1.dtype)
    w1s = (gsel[None, :, None, :, None, None] * pair[:, None, :, None, :, :]).reshape(
        2, NSA_GROUPS, CMP_STRIDE, LANES, 2 * CMP_HIDDEN)
    w2s = (cmp_w2[:, None, :, None, :] * gsel[None, :, None, :, None]).reshape(2, NSA_GROUPS, CMP_HIDDEN, LANES)
    pe = jnp.pad(cmp_pe.reshape(2, 1, CMP_LEN * HEAD_DIM), ((0, 0), (0, 7), (0, 0)))
    w1f = cmp_w1.reshape(2, CMP_LEN * HEAD_DIM, CMP_HIDDEN)
    return (w1s.astype(jnp.bfloat16), w2s.astype(jnp.bfloat16), pe.astype(jnp.bfloat16), w1f.astype(jnp.bfloat16))


def _compress_core(src_refs, n_chunk, w1s_ref, w2s_ref, pe_ref, w1f_ref):
    outs = []
    for z in range(2):
        const = _dot(pe_ref[z], w1f_ref[z])[0:1]
        acc = [jnp.zeros((n_chunk, 2 * CMP_HIDDEN), jnp.float32) for _ in range(NSA_GROUPS)]
        for rho in range(CMP_STRIDE):
            xr = src_refs[z][pl.ds(rho, n_chunk, stride=CMP_STRIDE), :].astype(jnp.bfloat16)
            for g in range(NSA_GROUPS):
                acc[g] = acc[g] + _dot(xr, w1s_ref[z, g, rho])
        out = jnp.zeros((n_chunk, LANES), jnp.float32)
        for g in range(NSA_GROUPS):
            nxt = pltpu.roll(acc[g][:, CMP_HIDDEN:], n_chunk - 1, 0)
            hid = _gelu_tanh(acc[g][:, :CMP_HIDDEN] + nxt + const)
            out = out + _dot(hid.astype(jnp.bfloat16), w2s_ref[z, g])
        outs.append(out)
    return outs


def _compress_body(k_ref, v_ref, w1s_ref, w2s_ref, pe_ref, w1f_ref, kc_ref, vc_ref):
    n_chunk = kc_ref.shape[0]
    kc, vc = _compress_core((k_ref, v_ref), n_chunk, w1s_ref, w2s_ref, pe_ref, w1f_ref)
    kc_ref[...] = kc.astype(kc_ref.dtype)
    vc_ref[...] = vc.astype(vc_ref.dtype)


def _full(a):
    nd = a.ndim
    return pl.BlockSpec(a.shape, lambda *_: (0,) * nd)


def _compress_prompt(kv32, cw, batch, seq):
    n_chunk = seq // CMP_STRIDE
    kv3 = kv32.reshape(batch, seq, KV_COLS)
    out = jax.ShapeDtypeStruct((batch, n_chunk, LANES), jnp.bfloat16)
    return pl.pallas_call(
        _compress_body,
        grid=(batch,),
        in_specs=[pl.BlockSpec((None, seq, LANES), lambda b: (b, 0, BLK_CMP_K)),
                  pl.BlockSpec((None, seq, LANES), lambda b: (b, 0, BLK_CMP_V))] + [_full(a) for a in cw],
        out_specs=[pl.BlockSpec((None, n_chunk, LANES), lambda b: (b, 0, 0))] * 2,
        out_shape=[out, out],
        compiler_params=_params("parallel"),
        name="compress",
    )(kv3, kv3, *cw)


def _overlap_consts(n_chunk, n_sel_lanes):
    cs = np.arange(n_chunk)[:, None] * CMP_STRIDE
    ss = np.arange(n_sel_lanes)[None, :] * SEL_LEN
    m = ((cs < ss + SEL_LEN) & (cs + CMP_LEN > ss)).astype(np.float32)
    if n_sel_lanes == LANES:
        return jnp.asarray(m[None], jnp.bfloat16)
    z = np.zeros_like(m)
    return jnp.asarray(np.stack([np.concatenate([m, z], 1), np.concatenate([z, m], 1)]), jnp.bfloat16)


def _upper_tri():
    j = np.arange(LANES)[:, None]
    s = np.arange(LANES)[None, :]
    return jnp.asarray((j > s).astype(np.float32), jnp.bfloat16)


def _flash_step(s, valid, v_tile, m_ref, l_ref, acc_ref, idx):
    s = jnp.where(valid, s, NEG_INF)
    m_old = m_ref[idx]
    m_new = jnp.maximum(m_old, jnp.max(s, axis=-1, keepdims=True))
    alpha = jnp.exp(m_old - m_new)
    p = jnp.where(valid, jnp.exp(s - m_new), 0.0)
    l_ref[idx] = alpha * l_ref[idx] + jnp.sum(p, axis=-1, keepdims=True)
    acc_ref[idx] = alpha * acc_ref[idx] + _dot(p.astype(v_tile.dtype), v_tile)
    m_ref[idx] = m_new


def _flash_init(m_ref, l_ref, acc_ref):
    m_ref[...] = jnp.full(m_ref.shape, NEG_INF, jnp.float32)
    l_ref[...] = jnp.zeros(l_ref.shape, jnp.float32)
    acc_ref[...] = jnp.zeros(acc_ref.shape, jnp.float32)


def _nsa_body(q_ref, kc_ref, vct_ref, ks_ref, vst_ref, kw_ref, vwt_ref, ng_ref, ovt_ref, o_ref,
              mem_ref, out_ref, m_ref, l_ref, acc_ref, *, tq, tk, n_cmp):
    i = pl.program_id(1)
    n_chunk = kc_ref.shape[0]
    q_pos = i * tq + _iota((1, tq), 1)
    gates = ng_ref[...].T

    nrow = _iota((n_chunk, 1), 0)
    d_cmp = q_pos - (nrow * CMP_STRIDE + (CMP_LEN - 1))
    valid_cmp = (d_cmp >= 0) & (nrow < n_cmp)
    dist_cmp = d_cmp.astype(jnp.float32)
    kc = kc_ref[...]
    vct = vct_ref[...]
    ps = jnp.zeros((LANES, tq), jnp.float32)
    for g in range(NSA_GROUPS):
        psum = jnp.zeros((n_chunk, tq), jnp.float32)
        for r in range(NSA_REP):
            h = g * NSA_REP + r
            s = _dot_nt(kc, q_ref[h]) - _SLOPES[h] * dist_cmp
            s = jnp.where(valid_cmp, s, NEG_INF)
            e = jnp.where(valid_cmp, jnp.exp(s - jnp.max(s, axis=0, keepdims=True)), 0.0)
            p = e / jnp.maximum(jnp.sum(e, axis=0, keepdims=True), 1e-30)
            psum = psum + p
            out_ref[h] = gates[3 * h:3 * h + 1, :] * _dot(vct, p.astype(vct.dtype))
        hi = psum.astype(jnp.bfloat16)
        lo = (psum - hi.astype(jnp.float32)).astype(jnp.bfloat16)
        ps = ps + _dot(ovt_ref[g], hi) + _dot(ovt_ref[g], lo)

    sub = _iota((LANES, 1), 0)
    blk = sub & (SEL_LEN - 1)
    cur = q_pos >> 6
    valid_blk = blk <= cur
    forced = (blk == 0) | (blk == cur) | (blk == cur - 1)
    score = jnp.where(valid_blk, ps + jnp.where(forced, FORCE_BONUS, 0.0), NEG_INF)
    low_half = sub < SEL_LEN
    rank = jnp.zeros((LANES, tq), jnp.float32)
    for c in range(SEL_LEN):
        other = jnp.where(low_half, score[c:c + 1, :], score[SEL_LEN + c:SEL_LEN + c + 1, :])
        beats = (other > score) | ((other == score) & (c < blk))
        rank = rank + jnp.where(beats, 1.0, 0.0)
    top_k = min(SEL_TOPK, ks_ref.shape[0] // SEL_LEN)
    member = jnp.where((rank < top_k) & valid_blk, 1.0, 0.0).astype(mem_ref.dtype)
    mem_ref[...] = jnp.concatenate([member] * NSA_REP, axis=1)

    wide = NSA_REP * tq
    col = _iota((1, wide), 1)
    rel = (col & (tq - 1)) - _iota((tk, 1), 0)
    rel_f = rel.astype(jnp.float32)
    causal = rel >= 0
    key_hi = jnp.where(_iota((tk, 1), 0) >= SEL_LEN, 1, 0)
    elane = _iota((tk, LANES), 1)

    def run(k_ref, vt_ref, g, kt_lo, use_sel):
        slope = jnp.zeros((1, wide), jnp.float32)
        for r in range(NSA_REP):
            slope = jnp.where((col >= r * tq) & (col < (r + 1) * tq), _SLOPES[g * NSA_REP + r], slope)
        bias = -(slope * rel_f)
        q4 = q_ref[g * NSA_REP:(g + 1) * NSA_REP].reshape(wide, LANES)
        m_ref[...] = jnp.full(m_ref.shape, NEG_INF, jnp.float32)
        l_ref[...] = jnp.zeros(l_ref.shape, jnp.float32)
        acc_ref[...] = jnp.zeros(acc_ref.shape, jnp.float32)

        def fetch(kt):
            off = pl.multiple_of(kt * tk, tk)
            s = _dot_nt(k_ref[pl.ds(off, tk), :], q4)
            if not use_sel:
                return (s,)
            expand = jnp.where(elane == g * SEL_LEN + kt * (tk // SEL_LEN) + key_hi, 1.0, 0.0)
            return s, _dot(expand.astype(mem_ref.dtype), mem_ref[...])

        def update(s, valid, kt):
            s = s + bias
            if valid is not None:
                s = jnp.where(valid, s, NEG_INF)
            shift = slope * ((kt - i) * tk).astype(jnp.float32)
            m_old = m_ref[...]
            m_new = jnp.maximum(m_old, jnp.max(s, axis=0, keepdims=True) + shift)
            alpha = jnp.exp(m_old - m_new)
            p = jnp.exp(s - (m_new - shift))
            l_ref[...] = alpha * l_ref[...] + jnp.sum(p, axis=0, keepdims=True)
            off = pl.multiple_of(kt * tk, tk)
            acc_ref[...] = alpha * acc_ref[...] + _dot(vt_ref[:, pl.ds(off, tk)], p.astype(vt_ref.dtype))
            m_ref[...] = m_new

        def mask(pk, kt, diag):
            if use_sel:
                picked = pk[0] > 0.5
                return picked & causal if diag else picked
            d = (i - kt) * tk + rel
            return (d >= 0) & (d < WINDOW)

        first = fetch(i)
        update(first[0], mask(first[1:], i, True), i)

        def step(t, carry):
            kt = i - t
            nxt = fetch(jnp.maximum(kt - 1, 0))
            update(carry[0], mask(carry[1:], kt, False), kt)
            return nxt

        lax.fori_loop(1, i - kt_lo + 1, step, fetch(jnp.maximum(i - 1, 0)))

    kt_win = jnp.maximum(i * tq - (WINDOW - 1), 0) // tk
    for g in range(NSA_GROUPS):
        for branch, (k_ref, vt_ref, lo, use_sel) in enumerate(
                [(ks_ref, vst_ref, 0, True), (kw_ref, vwt_ref, kt_win, False)]):
            run(k_ref, vt_ref, g, lo, use_sel)
            o = acc_ref[...] / jnp.maximum(l_ref[...], 1e-30)
            for r in range(NSA_REP):
                h = g * NSA_REP + r
                gate = gates[3 * h + 1 + branch:3 * h + 2 + branch, :]
                out_ref[h] = out_ref[h] + gate * o[:, r * tq:(r + 1) * tq]
    lane = _iota((tq, LANES), 1)
    for h in range(NSA_HEADS):
        keep = (lane >= SEL_LEN) == (h >= NSA_REP)
        o_ref[:, h * LANES:(h + 1) * LANES] = jnp.where(keep, out_ref[h].T, 0.0).astype(o_ref.dtype)


def _nsa_prompt(qn, kvb, winb, kc, vc, ng, batch, seq):
    tq = tk = 128
    n_chunk = seq // CMP_STRIDE
    n_cmp = (seq - CMP_LEN) // CMP_STRIDE + 1
    assert seq % tq == 0 and seq // SEL_LEN <= SEL_LEN
    ovt = jnp.swapaxes(_overlap_consts(n_chunk, SEL_LEN), 1, 2)
    kv3 = kvb.reshape(batch, seq, KV_COLS)
    win3 = winb.reshape(batch, seq, WIN_COLS)
    vct = jnp.swapaxes(vc, 1, 2)
    vst = jnp.swapaxes(kv3[:, :, BLK_SEL_V * LANES:(BLK_SEL_V + 1) * LANES], 1, 2)
    vwt = jnp.swapaxes(win3[:, :, LANES:2 * LANES], 1, 2)

    def keys_blk(c):
        return pl.BlockSpec((None, seq, LANES), lambda b, i: (b, 0, c))

    def vals_blk(n):
        return pl.BlockSpec((None, LANES, n), lambda b, i: (b, 0, 0))

    nq = seq // tq
    return pl.pallas_call(
        functools.partial(_nsa_body, tq=tq, tk=tk, n_cmp=n_cmp),
        grid=(batch, nq),
        in_specs=[pl.BlockSpec((NSA_HEADS, tq, LANES), lambda b, i: (0, b * nq + i, 0)),
                  pl.BlockSpec((None, n_chunk, LANES), lambda b, i: (b, 0, 0)), vals_blk(n_chunk),
                  keys_blk(BLK_SEL_K), vals_blk(seq),
                  keys_blk(0), vals_blk(seq),
                  pl.BlockSpec((tq, LANES), lambda b, i: (b * nq + i, 0)),
                  _full(ovt)],
        out_specs=pl.BlockSpec((tq, QN_PAD_COLS), lambda b, i: (b * nq + i, 0)),
        out_shape=jax.ShapeDtypeStruct((batch * seq, QN_PAD_COLS), jnp.bfloat16),
        scratch_shapes=[pltpu.VMEM((LANES, NSA_REP * tq), jnp.bfloat16),
                        pltpu.VMEM((NSA_HEADS, LANES, tq), jnp.float32),
                        pltpu.VMEM((1, NSA_REP * tq), jnp.float32),
                        pltpu.VMEM((1, NSA_REP * tq), jnp.float32),
                        pltpu.VMEM((LANES, NSA_REP * tq), jnp.float32)],
        compiler_params=_params("parallel", "parallel"),
        name="nsa",
    )(qn, kc, vct, kv3, vst, win3, vwt, ng, ovt)


def _softplus(z):
    return jnp.maximum(z, 0.0) + jnp.log1p(jnp.exp(-jnp.abs(z)))


EXP_UNDERFLOW = -104.0


def _sb_body(q_ref, k_ref, vt_ref, ut_ref, o_ref, *, tq, tk):
    i = pl.program_id(2)
    lane = _iota((tq, LANES), 1)
    wide = 2 * tq
    strictly_past = ((_iota((1, wide), 1) & (tq - 1)) - _iota((tk, 1), 0)) > 0
    q2 = q_ref[...].astype(jnp.float32)
    qq = jnp.concatenate([jnp.where((lane >= HEAD_DIM) == (e == 1), q2, 0.0).astype(jnp.bfloat16)
                          for e in range(2)], axis=0)
    ut = ut_ref[...]

    def fetch(kt):
        return _dot_nt(k_ref[pl.ds(pl.multiple_of(kt * tk, tk), tk), :], qq)

    def update(z, kt, rest_later, acc, past):
        sp = _softplus(z)
        log_keep = -sp if past is None else jnp.where(past, -sp, 0.0)
        hi = log_keep.astype(jnp.bfloat16)
        lo = (log_keep - hi.astype(jnp.float32)).astype(jnp.bfloat16)
        rest2 = _dot(ut, jnp.concatenate([hi, lo], axis=1))
        rest = rest2[:, :wide] + rest2[:, wide:]
        a = jnp.exp((z - sp) + rest + rest_later)
        if past is not None:
            a = jnp.where(past, a, 0.0)
        acc = acc + _dot(vt_ref[:, pl.ds(pl.multiple_of(kt * tk, tk), tk)], a.astype(vt_ref.dtype))
        return rest_later + (rest[0:1, :] + log_keep[0:1, :]), acc

    rest_later, acc = update(fetch(i), i, jnp.zeros((1, wide), jnp.float32),
                             jnp.zeros((LANES, wide), jnp.float32), strictly_past)

    def cond(c):
        return (c[0] <= i) & (c[1] > EXP_UNDERFLOW)

    def step(c):
        t, _, z, rest_later, acc = c
        kt = i - t
        z_next = fetch(jnp.maximum(kt - 1, 0))
        rest_later, acc = update(z, kt, rest_later, acc, None)
        return t + 1, jnp.max(rest_later), z_next, rest_later, acc

    out = lax.while_loop(cond, step, (jnp.int32(1), jnp.max(rest_later), fetch(jnp.maximum(i - 1, 0)),
                                      rest_later, acc))
    acc = out[4]
    sub = _iota((LANES, 1), 0)
    o_ref[...] = jnp.where(sub < HEAD_DIM, acc[:, :tq], acc[:, tq:]).T.astype(o_ref.dtype)


def _sb_prompt(qs, kvb, batch, seq):
    tq = tk = 128
    assert seq % tq == 0
    kv3 = kvb.reshape(batch, seq, KV_COLS)
    nq = seq // tq
    n_pair = SB_HEADS // 2
    ut = _upper_tri().T
    vt = jnp.swapaxes(kv3[:, :, BLK_SB_V * LANES:], 1, 2)
    return pl.pallas_call(
        functools.partial(_sb_body, tq=tq, tk=tk),
        grid=(batch, n_pair, nq),
        in_specs=[pl.BlockSpec((tq, LANES), lambda b, p, i: (b * nq + i, p)),
                  pl.BlockSpec((None, seq, LANES), lambda b, p, i: (b, 0, BLK_SB_K + p)),
                  pl.BlockSpec((None, LANES, seq), lambda b, p, i: (b, p, 0)),
                  _full(ut)],
        out_specs=pl.BlockSpec((tq, LANES), lambda b, p, i: (b * nq + i, p)),
        out_shape=jax.ShapeDtypeStruct((batch * seq, Q_SB_COLS), jnp.bfloat16),
        compiler_params=_params("parallel", "parallel", "parallel"),
        name="sb",
    )(qs, kv3, vt, ut)


def _sample_a_body(pt_ref, pool_ref, q_ref, w1s_ref, w2s_ref, pe_ref, w1f_ref, ov_ref, ocmp_ref, mem_ref, past_ref,
                   *, page, n_pages):
    del pt_ref
    j = pl.program_id(1)
    rows = pl.ds(pl.multiple_of(j * page, page), page)
    past_ref[0, rows, :] = pool_ref[:, BLK_CMP_K * LANES:(BLK_CMP_K + 1) * LANES]
    past_ref[1, rows, :] = pool_ref[:, BLK_CMP_V * LANES:(BLK_CMP_V + 1) * LANES]

    @pl.when(j == n_pages - 1)
    def _():
        past_len = n_pages * page
        n_chunk = past_len // CMP_STRIDE
        n_cmp = (past_len + 1 - CMP_LEN) // CMP_STRIDE + 1
        n_past_blk = past_len // SEL_LEN
        kc, vc = _compress_core((past_ref.at[0], past_ref.at[1]), n_chunk, w1s_ref, w2s_ref, pe_ref, w1f_ref)
        kc = kc.astype(jnp.bfloat16)
        vc = vc.astype(jnp.bfloat16)
        q8 = q_ref[...]
        ncol = _iota((NSA_HEADS, n_chunk), 1)
        d = past_len - (ncol * CMP_STRIDE + (CMP_LEN - 1))
        valid = (d >= 0) & (ncol < n_cmp)
        hrow = _iota((NSA_HEADS, 1), 0)
        s = _dot_nt(q8, kc) - _slope_col() * d.astype(jnp.float32)
        s = jnp.where(valid, s, NEG_INF)
        e = jnp.where(valid, jnp.exp(s - jnp.max(s, axis=-1, keepdims=True)), 0.0)
        p = e / jnp.maximum(jnp.sum(e, axis=-1, keepdims=True), 1e-30)
        ocmp_ref[...] = _dot(p.astype(vc.dtype), vc)
        psum = jnp.zeros_like(p)
        for g in range(NSA_GROUPS):
            pg = jnp.sum(p[g * NSA_REP:(g + 1) * NSA_REP], axis=0, keepdims=True)
            psum = jnp.where((hrow >= g * NSA_REP) & (hrow < (g + 1) * NSA_REP), pg, psum)
        ps = _dot_split(psum, ov_ref[0])
        blk = _iota((NSA_HEADS, LANES), 1)
        forced = (blk == 0) | (blk == n_past_blk - 1)
        score = jnp.where(blk < n_past_blk, ps + jnp.where(forced, FORCE_BONUS, 0.0), NEG_INF)
        eye = _iota((LANES, LANES), 0) == _iota((LANES, LANES), 1)
        before = _iota((LANES, LANES), 0) < _iota((LANES, LANES), 1)
        for g in range(NSA_GROUPS):
            srow = score[g * NSA_REP:g * NSA_REP + 1, :]
            scol = jnp.sum(jnp.where(eye, srow, 0.0), axis=1, keepdims=True)
            beats = (scol > srow) | ((scol == srow) & before)
            rank = jnp.sum(jnp.where(beats, 1.0, 0.0), axis=0, keepdims=True)
            top_k = min(SEL_TOPK, n_past_blk + 1)
            member = jnp.where((rank < top_k - 1) & (blk[0:1] < n_past_blk), 1.0, 0.0)
            mem_ref[g * NSA_REP:(g + 1) * NSA_REP, :] = jnp.broadcast_to(member, (NSA_REP, LANES))


def _sample_a(pool3, pt_flat, q8, cw, dec, n_pages, page):
    past_len = n_pages * page
    n_chunk = past_len // CMP_STRIDE
    assert past_len // SEL_LEN <= LANES and past_len % SEL_LEN == 0
    ov = _overlap_consts(n_chunk, LANES)
    o8 = jax.ShapeDtypeStruct((dec, NSA_HEADS, LANES), jnp.float32)

    def const(a):
        nd = a.ndim
        return pl.BlockSpec(a.shape, lambda b, j, pt: (0,) * nd)

    grid_spec = pltpu.PrefetchScalarGridSpec(
        num_scalar_prefetch=1,
        grid=(dec, n_pages),
        in_specs=[pl.BlockSpec((None, page, 2 * LANES), lambda b, j, pt: (pt[b * n_pages + j], 0, 0)),
                  pl.BlockSpec((None, NSA_HEADS, LANES), lambda b, j, pt: (b, 0, 0))]
                 + [const(a) for a in cw] + [const(ov)],
        out_specs=[pl.BlockSpec((None, NSA_HEADS, LANES), lambda b, j, pt: (b, 0, 0))] * 2,
        scratch_shapes=[pltpu.VMEM((2, past_len, LANES), jnp.float32)],
    )
    return pl.pallas_call(
        functools.partial(_sample_a_body, page=page, n_pages=n_pages),
        grid_spec=grid_spec,
        out_shape=[o8, o8],
        compiler_params=_params("parallel", "arbitrary"),
        name="sample_a",
    )(pt_flat, pool3, q8, *cw, ov)


def _sample_b_body(pt_ref, pool_ref, qn_ref, qs_ref, kvn_ref, winn_ref, cw_ref, mem_ref, ocmp_ref, gate_ref, u_ref,
                   oa_ref, ob_ref, wout_ref, m_ref, l_ref, acc_ref, rest_ref, sbacc_ref, *, page, n_pages, wbuf):
    del pt_ref
    jj = pl.program_id(1)
    j = n_pages - 1 - jj
    past_len = n_pages * page
    q8 = qn_ref[...]
    hrow = _iota((NSA_HEADS, 1), 0)
    slope = _slope_col()
    lane8 = _iota((NSA_HEADS, LANES), 1)
    own_half = (lane8 >= HEAD_DIM) == (hrow >= NSA_REP)

    def new_row(ref, c):
        return ref[:, c * LANES:(c + 1) * LANES].astype(jnp.bfloat16).astype(jnp.float32)

    @pl.when(jj == 0)
    def _():
        qf = q8.astype(jnp.float32)
        m_ref[0] = jnp.sum(qf * new_row(kvn_ref, BLK_SEL_K), axis=-1, keepdims=True)
        l_ref[0] = jnp.ones((NSA_HEADS, 1), jnp.float32)
        acc_ref[0] = jnp.broadcast_to(new_row(kvn_ref, BLK_SEL_V), (NSA_HEADS, LANES))
        rest_ref[...] = jnp.zeros(rest_ref.shape, jnp.float32)
        sbacc_ref[...] = jnp.zeros(sbacc_ref.shape, jnp.float32)

    kcol = _iota((NSA_HEADS, page), 1)
    d = past_len - (j * page + kcol)
    expand = jnp.where(_iota((LANES, page), 0) == j * (page // SEL_LEN) + (_iota((LANES, page), 1) >> 6), 1.0, 0.0)
    picked = _dot(mem_ref[...].astype(jnp.bfloat16), expand.astype(jnp.bfloat16))
    k_sel = pool_ref[:, BLK_SEL_K * LANES:(BLK_SEL_K + 1) * LANES].astype(jnp.bfloat16)
    v_sel = pool_ref[:, BLK_SEL_V * LANES:(BLK_SEL_V + 1) * LANES].astype(jnp.bfloat16)
    s = _dot_nt(q8, k_sel) - slope * d.astype(jnp.float32)
    _flash_step(s, picked > 0.5, v_sel, m_ref, l_ref, acc_ref, 0)

    qs = qs_ref[...]
    lane_sb = _iota((SB_HEADS, Q_SB_COLS), 1)
    head_sb = _iota((SB_HEADS, Q_SB_COLS), 0)
    own_sb = (lane_sb >> 6) == head_sb
    qbd = jnp.where(own_sb, qs.astype(jnp.float32), 0.0).astype(jnp.bfloat16)
    k_sb = pool_ref[:, BLK_SB_K * LANES:BLK_SB_V * LANES].astype(jnp.bfloat16)
    v_sb = pool_ref[:, BLK_SB_V * LANES:KV_COLS].astype(jnp.bfloat16)
    z = _dot_nt(qbd, k_sb)
    sp = _softplus(z)
    log_keep = -sp
    rest = _dot_split(log_keep, u_ref[...])
    a = jnp.exp((z - sp) + rest + rest_ref[...])
    sbacc_ref[...] = sbacc_ref[...] + _dot(a.astype(jnp.bfloat16), v_sb)
    rest_ref[...] = rest_ref[...] + (rest[:, 0:1] + log_keep[:, 0:1])

    @pl.when(jj == n_pages - 1)
    def _():
        o_sel = acc_ref[0] / jnp.maximum(l_ref[0], 1e-30)
        cw = cw_ref[...]
        k_w = cw[:, 0:LANES].astype(jnp.bfloat16)
        v_w = cw[:, LANES:2 * LANES].astype(jnp.bfloat16)
        wcol = _iota((NSA_HEADS, wbuf), 1)
        dw = wbuf - wcol
        valid_w = (dw < WINDOW) & (past_len - dw >= 0)
        s_w = jnp.where(valid_w, _dot_nt(q8, k_w) - slope * dw.astype(jnp.float32), NEG_INF)
        s_n = jnp.sum(q8.astype(jnp.float32) * new_row(winn_ref, 0), axis=-1, keepdims=True)
        m_w = jnp.maximum(jnp.max(s_w, axis=-1, keepdims=True), s_n)
        p_w = jnp.where(valid_w, jnp.exp(s_w - m_w), 0.0)
        p_n = jnp.exp(s_n - m_w)
        l_w = jnp.sum(p_w, axis=-1, keepdims=True) + p_n
        o_win = (_dot(p_w.astype(jnp.bfloat16), v_w)
                 + p_n.astype(jnp.bfloat16).astype(jnp.float32) * new_row(winn_ref, 1)) / jnp.maximum(l_w, 1e-30)
        gate = gate_ref[...]
        o_a = gate[:, 0:1] * ocmp_ref[...] + gate[:, 1:2] * o_sel + gate[:, 2:3] * o_win
        oa_ref[...] = jnp.where(own_half, o_a, 0.0)
        ob_ref[...] = jnp.sum(jnp.where(own_sb, sbacc_ref[...], 0.0), axis=0, keepdims=True)
        wrow = _iota((wbuf, 1), 0)
        wout_ref[...] = jnp.where(wrow == wbuf - 1, winn_ref[...], pltpu.roll(cw, wbuf - 1, 0))


def _sample_b(pool3, pt_flat, q8, qs, kv32, win32, cache_win2, member, ocmp, gate, dec, n_pages, page):
    wbuf = cache_win2.shape[1]
    u = _upper_tri()
    assert page == LANES

    def per_seq(*shape):
        nd = len(shape)
        return pl.BlockSpec((None,) + shape, lambda b, jj, pt: (b,) + (0,) * nd)

    grid_spec = pltpu.PrefetchScalarGridSpec(
        num_scalar_prefetch=1,
        grid=(dec, n_pages),
        in_specs=[pl.BlockSpec((None, page, KV_COLS), lambda b, jj, pt: (pt[b * n_pages + n_pages - 1 - jj], 0, 0)),
                  per_seq(NSA_HEADS, LANES), per_seq(1, Q_SB_COLS), per_seq(1, KV_COLS), per_seq(1, WIN_COLS),
                  per_seq(wbuf, WIN_COLS), per_seq(NSA_HEADS, LANES), per_seq(NSA_HEADS, LANES),
                  per_seq(NSA_HEADS, 3),
                  pl.BlockSpec(u.shape, lambda b, jj, pt: (0, 0))],
        out_specs=[per_seq(NSA_HEADS, LANES), per_seq(1, Q_SB_COLS), per_seq(wbuf, WIN_COLS)],
        scratch_shapes=[pltpu.VMEM((1, NSA_HEADS, 1), jnp.float32), pltpu.VMEM((1, NSA_HEADS, 1), jnp.float32),
                        pltpu.VMEM((1, NSA_HEADS, LANES), jnp.float32), pltpu.VMEM((SB_HEADS, 1), jnp.float32),
                        pltpu.VMEM((SB_HEADS, Q_SB_COLS), jnp.float32)],
    )
    return pl.pallas_call(
        functools.partial(_sample_b_body, page=page, n_pages=n_pages, wbuf=wbuf),
        grid_spec=grid_spec,
        out_shape=[jax.ShapeDtypeStruct((dec, NSA_HEADS, LANES), jnp.float32),
                   jax.ShapeDtypeStruct((dec, 1, Q_SB_COLS), jnp.float32),
                   jax.ShapeDtypeStruct((dec, wbuf, WIN_COLS), jnp.float32)],
        compiler_params=_params("parallel", "arbitrary"),
        name="sample_b",
    )(pt_flat, pool3, q8, qs, kv32, win32, cache_win2, member, ocmp, gate, u)


def _post_body(x_ref, oa_ref, ob_ref, mg_ref, wa_ref, wb_ref, wo_ref, g_ref, y_ref):
    d = x_ref.shape[1]
    u_a = _dot(oa_ref[...].astype(jnp.bfloat16), wa_ref[...])
    u_b = _dot(ob_ref[...].astype(jnp.bfloat16), wb_ref[...])
    mixed = mg_ref[:, 0:d].astype(jnp.float32) * u_a + mg_ref[:, d:2 * d].astype(jnp.float32) * u_b
    y_ref[...] = x_ref[...] + _rms(_dot(mixed.astype(jnp.bfloat16), wo_ref[...]), g_ref[...])


def _post(x2, oa, ob, mg, wa, wb, wo, g):
    m, d = x2.shape
    tm = min(256, m)

    def rows(c):
        return pl.BlockSpec((tm, c), lambda i: (i, 0))

    return pl.pallas_call(
        _post_body,
        grid=(m // tm,),
        in_specs=[rows(d), rows(oa.shape[1]), rows(ob.shape[1]), rows(mg.shape[1]),
                  _full(wa), _full(wb), _full(wo), _full(g)],
        out_specs=rows(d),
        out_shape=jax.ShapeDtypeStruct((m, d), jnp.float32),
        compiler_params=_params("parallel"),
        name="post",
    )(x2, oa, ob, mg, wa, wb, wo, g)


def _ffn_body(x_ref, prev_ref, g2_ref, g3_ref, wup_ref, cw_ref, cb_ref, wdn_ref, y_ref, st_ref, carry_ref,
              *, tm, d_ff, fc, per_row):
    t = pl.program_id(1)
    x = x_ref[...]
    h = _rms(x, g2_ref[...]).astype(jnp.bfloat16)
    if not per_row:
        @pl.when(t == 0)
        def _():
            carry_ref[...] = prev_ref[...]
    row = _iota((tm, 1), 0)
    acc = jnp.zeros((tm, x.shape[1]), jnp.float32)
    for c0 in range(0, d_ff, fc):
        halves = []
        for base in (c0, d_ff + c0):
            cols = slice(base, base + fc)
            u = _dot(h, wup_ref[:, cols])
            if per_row:
                u1 = prev_ref[1, :, cols]
                u2 = prev_ref[0, :, cols]
                st_ref[0, :, cols] = u1
                st_ref[1, :, cols] = u
            else:
                p1 = carry_ref[1:2, cols]
                p2 = carry_ref[0:1, cols]
                u1 = jnp.where(row < 1, p1, pltpu.roll(u, 1, 0))
                u2 = jnp.where(row < 1, p2, jnp.where(row < 2, p1, pltpu.roll(u, 2, 0)))
                carry_ref[:, cols] = u[tm - 2:tm, :]
                st_ref[:, cols] = u[tm - 2:tm, :]
            halves.append(cb_ref[:, cols] + u2 * cw_ref[0:1, cols] + u1 * cw_ref[1:2, cols] + u * cw_ref[2:3, cols])
        act = (_gelu_tanh(halves[0]) * halves[1]).astype(jnp.bfloat16)
        acc = acc + _dot(act, wdn_ref[c0:c0 + fc, :])
    y_ref[...] = x + _rms(acc, g3_ref[...])


def _ffn(x2, prev, g2, g3, wup, cw, cb, wdn, batch, seq, per_row):
    m, d = x2.shape
    d_ff = wdn.shape[0]
    fc = 256
    assert d_ff % fc == 0
    if per_row:
        tm, nb, nt = m, 1, 1
        prev_spec = pl.BlockSpec((2, tm, 2 * d_ff), lambda b, t: (0, 0, 0))
        st_spec = pl.BlockSpec((2, tm, 2 * d_ff), lambda b, t: (0, 0, 0))
        st_shape = jax.ShapeDtypeStruct((2, m, 2 * d_ff), jnp.float32)
    else:
        tm = min(256, seq)
        nb, nt = batch, seq // tm
        prev_spec = pl.BlockSpec((None, 2, 2 * d_ff), lambda b, t: (b, 0, 0))
        st_spec = pl.BlockSpec((None, 2, 2 * d_ff), lambda b, t: (b, 0, 0))
        st_shape = jax.ShapeDtypeStruct((batch, 2, 2 * d_ff), jnp.float32)
    return pl.pallas_call(
        functools.partial(_ffn_body, tm=tm, d_ff=d_ff, fc=fc, per_row=per_row),
        grid=(nb, nt),
        in_specs=[pl.BlockSpec((tm, d), lambda b, t: (b * nt + t, 0)), prev_spec,
                  _full(g2), _full(g3), _full(wup), _full(cw), _full(cb), _full(wdn)],
        out_specs=[pl.BlockSpec((tm, d), lambda b, t: (b * nt + t, 0)), st_spec],
        out_shape=[jax.ShapeDtypeStruct((m, d), jnp.float32), st_shape],
        scratch_shapes=[pltpu.VMEM((2, 2 * d_ff), jnp.float32)],
        compiler_params=_params("parallel", "arbitrary"),
        name="ffn",
    )(x2, prev, g2, g3, wup, cw, cb, wdn)


def _branch_a_weights(w_branch_a):
    d = w_branch_a.shape[1]
    w = w_branch_a.reshape(NSA_GROUPS, NSA_REP, 1, HEAD_DIM, d)
    half = jnp.eye(NSA_GROUPS, dtype=w.dtype)
    return (w * half[:, None, :, None, None]).reshape(QN_PAD_COLS, d).astype(jnp.bfloat16)


def kernel(x_prompt, x_sample, cache_kv, cache_win, state_conv, page_table, norm_g, w_in, cmp_w1, cmp_w2, cmp_pe,
           w_branch_a, w_branch_b, w_out, ffn_w_up, ffn_conv_w, ffn_conv_b, ffn_w_down):
    depth = norm_g.shape[0]
    assert depth == 1, "one layer per call: outputs of deeper stacks are not wired"
    batch, seq, d_model = x_prompt.shape
    dec, dec_seq, _ = x_sample.shape
    assert dec_seq == 1
    n_pool, page = cache_kv.shape[1], cache_kv.shape[2]
    n_pages = page_table.shape[1]
    wbuf = cache_win.shape[2]
    d_ff = ffn_w_down.shape[1]
    l = 0

    g = norm_g[l].reshape(4, 1, d_model)
    wp = _proj_weights(w_in[l], d_model)
    cw = _compress_weights(cmp_w1[l], cmp_w2[l], cmp_pe[l])
    wa = _branch_a_weights(w_branch_a[l])
    wb = w_branch_b[l].astype(jnp.bfloat16)
    wo = w_out[l].astype(jnp.bfloat16)
    wup = ffn_w_up[l].astype(jnp.bfloat16)
    wdn = ffn_w_down[l].astype(jnp.bfloat16)
    conv_w = ffn_conv_w[l]
    conv_b = ffn_conv_b[l].reshape(1, 2 * d_ff)

    xp = x_prompt.reshape(batch * seq, d_model)
    qn, kv32, kvb, win32, winb, qs, ng, mg = _proj(xp, g[0], wp)
    kc, vc = _compress_prompt(kv32, cw, batch, seq)
    oa = _nsa_prompt(qn, kvb, winb, kc, vc, ng, batch, seq)
    ob = _sb_prompt(qs, kvb, batch, seq)
    x1 = _post(xp, oa, ob, mg, wa, wb, wo, g[1])
    conv0 = jnp.zeros((batch, CONV_W - 1, 2 * d_ff), jnp.float32)
    yp, conv_p = _ffn(x1, conv0, g[2], g[3], wup, conv_w, conv_b, wdn, batch, seq, per_row=False)
    kv_p = kv32.reshape(1, batch, seq, KV_ROWS, HEAD_DIM)
    win_seq = win32.reshape(batch, seq, WIN_ROWS, HEAD_DIM)
    win_p = jnp.pad(win_seq, ((0, 0), (wbuf, 0), (0, 0), (0, 0)))[:, -wbuf:][None]

    xs = x_sample.reshape(dec, d_model)
    qn_s, kv32_s, _, win32_s, _, qs_s, ng_s, mg_s = _proj(xs, g[0], wp)
    pool3 = cache_kv[l].reshape(n_pool, page, KV_COLS)
    pt_flat = page_table.reshape(dec * n_pages)
    q8 = jnp.swapaxes(qn_s, 0, 1)
    ocmp, member = _sample_a(pool3, pt_flat, q8, cw, dec, n_pages, page)
    gate = ng_s[:, :NSA_GATE_COLS].reshape(dec, NSA_HEADS, 3)
    oa_s, ob_s, win_s = _sample_b(pool3, pt_flat, q8, qs_s.reshape(dec, 1, Q_SB_COLS),
                                  kv32_s.reshape(dec, 1, KV_COLS), win32_s.reshape(dec, 1, WIN_COLS),
                                  cache_win[l].reshape(dec, wbuf, WIN_COLS), member, ocmp, gate, dec, n_pages, page)
    x1_s = _post(xs, oa_s.reshape(dec, QN_PAD_COLS), ob_s.reshape(dec, Q_SB_COLS), mg_s, wa, wb, wo, g[1])
    prev_s = jnp.swapaxes(state_conv[l], 0, 1)
    ys, conv_s = _ffn(x1_s, prev_s, g[2], g[3], wup, conv_w, conv_b, wdn, dec, 1, per_row=True)

    return (yp.reshape(batch, seq, d_model), ys.reshape(dec, 1, d_model), kv_p,
            kv32_s.reshape(1, dec, 1, KV_ROWS, HEAD_DIM), win_p,
            win_s.reshape(1, dec, wbuf, WIN_ROWS, HEAD_DIM), conv_p[None],
            jnp.swapaxes(conv_s, 0, 1)[None])
```

```python
import functools
import math

import numpy as np
import jax
import jax.numpy as jnp
from jax import lax
from jax.experimental import pallas as pl
from jax.experimental.pallas import tpu as pltpu

HEAD_DIM = 64
NSA_HEADS = 8
NSA_GROUPS = 2
NSA_REP = NSA_HEADS // NSA_GROUPS
SB_HEADS = 8
CMP_LEN = 32
CMP_STRIDE = 16
CMP_HIDDEN = 128
SEL_LEN = 64
SEL_TOPK = 16
WINDOW = 512
CONV_W = 3
RMS_EPS = 1e-6
NEG_INF = -1e30
FORCE_BONUS = 1e6
ATTN_SCALE = HEAD_DIM ** -0.5
KV_ROWS = 4 * NSA_GROUPS + 2 * SB_HEADS
WIN_ROWS = 2 * NSA_GROUPS

LANES = 128
Q_NSA_COLS = NSA_HEADS * HEAD_DIM
KV_COLS = KV_ROWS * HEAD_DIM
WIN_COLS = WIN_ROWS * HEAD_DIM
Q_SB_COLS = SB_HEADS * HEAD_DIM
NSA_GATE_COLS = 3 * NSA_HEADS
QN_PAD_COLS = NSA_HEADS * LANES

BLK_CMP_K, BLK_CMP_V, BLK_SEL_K, BLK_SEL_V = 0, 1, 2, 3
BLK_SB_K = 4
BLK_SB_V = 4 + SB_HEADS // 2

VMEM_LIMIT = 56 * 1024 * 1024

_SLOPES = [float(np.power(np.float32(2.0), np.float32(-8.0 * (h + 1) / NSA_HEADS))) for h in range(NSA_HEADS)]


def _dot(a, b):
    return jnp.dot(a, b, preferred_element_type=jnp.float32)


def _dot_nt(a, b):
    return lax.dot_general(a, b, (((1,), (1,)), ((), ())), preferred_element_type=jnp.float32)


def _dot_split(x, w):
    hi = x.astype(jnp.bfloat16)
    lo = (x - hi.astype(jnp.float32)).astype(jnp.bfloat16)
    return _dot(hi, w) + _dot(lo, w)


def _rms(x, g):
    return x * lax.rsqrt(jnp.mean(x * x, axis=-1, keepdims=True) + RMS_EPS) * g


def _gelu_tanh(x):
    return 0.5 * x * (1.0 + jnp.tanh(math.sqrt(2.0 / math.pi) * (x + 0.044715 * (x * x * x))))


def _sigmoid(x):
    return 1.0 / (1.0 + jnp.exp(-x))


def _iota(shape, dim):
    return lax.broadcasted_iota(jnp.int32, shape, dim)


def _slope_col():
    hrow = _iota((NSA_HEADS, 1), 0)
    out = jnp.zeros((NSA_HEADS, 1), jnp.float32)
    for h in range(NSA_HEADS):
        out = jnp.where(hrow == h, _SLOPES[h], out)
    return out


def _params(*sem):
    return pltpu.CompilerParams(dimension_semantics=sem, vmem_limit_bytes=VMEM_LIMIT)


_SEG_QN = (0, QN_PAD_COLS)
_SEG_KV = (_SEG_QN[1], _SEG_QN[1] + KV_COLS)
_SEG_WIN = (_SEG_KV[1], _SEG_KV[1] + WIN_COLS)
_SEG_QS = (_SEG_WIN[1], _SEG_WIN[1] + Q_SB_COLS)
_SEG_NG = (_SEG_QS[1], _SEG_QS[1] + LANES)
PROJ_PAD_COLS_FIXED = _SEG_NG[1]


def _proj_weights(w_in, d_model):
    o1 = Q_NSA_COLS
    o2 = o1 + KV_COLS
    o3 = o2 + WIN_COLS
    o4 = o3 + Q_SB_COLS
    o5 = o4 + NSA_GATE_COLS
    wq = (w_in[:, :o1] * ATTN_SCALE).reshape(d_model, NSA_GROUPS, NSA_REP, 1, HEAD_DIM)
    half = (jnp.arange(NSA_GROUPS)[:, None] == jnp.arange(2)[None, :]).astype(w_in.dtype)
    wq = (wq * half[None, :, None, :, None]).reshape(d_model, QN_PAD_COLS)
    ng = jnp.pad(w_in[:, o4:o5], ((0, 0), (0, LANES - NSA_GATE_COLS)))
    w = jnp.concatenate([wq, w_in[:, o1:o3], w_in[:, o3:o4] * ATTN_SCALE, ng, w_in[:, o5:]], axis=1)
    return w.astype(jnp.bfloat16)


def _proj_body(x_ref, g_ref, w_ref, qn_ref, kv_ref, kvb_ref, win_ref, winb_ref, qs_ref, ng_ref, mg_ref):
    h = _rms(x_ref[...], g_ref[...]).astype(jnp.bfloat16)

    def seg(c0, c1, fn, step=512):
        for a in range(c0, c1, step):
            b = min(a + step, c1)
            fn(a - c0, b - c0, _dot(h, w_ref[:, a:b]))

    def put_qn(a, b, r):
        for c in range(a, b, LANES):
            qn_ref[c // LANES] = r[:, c - a:c - a + LANES].astype(qn_ref.dtype)

    def put_kv(a, b, r):
        kv_ref[:, a:b] = r
        kvb_ref[:, a:b] = r.astype(kvb_ref.dtype)

    def put_win(a, b, r):
        win_ref[:, a:b] = r
        winb_ref[:, a:b] = r.astype(winb_ref.dtype)

    def put_qs(a, b, r):
        qs_ref[:, a:b] = r.astype(qs_ref.dtype)

    def put_ng(a, b, r):
        ng_ref[:, a:b] = _sigmoid(r)

    def put_mg(a, b, r):
        mg_ref[:, a:b] = _sigmoid(r).astype(mg_ref.dtype)

    seg(*_SEG_QN, put_qn)
    seg(*_SEG_KV, put_kv)
    seg(*_SEG_WIN, put_win)
    seg(*_SEG_QS, put_qs)
    seg(*_SEG_NG, put_ng)
    seg(PROJ_PAD_COLS_FIXED, w_ref.shape[1], put_mg)


def _proj(x2, g, w):
    m, d = x2.shape
    tm = min(256, m)
    n = w.shape[1]
    n_mg = n - PROJ_PAD_COLS_FIXED

    def rows(c):
        return pl.BlockSpec((tm, c), lambda i: (i, 0))

    out_cols = [(KV_COLS, jnp.float32), (KV_COLS, jnp.bfloat16),
                (WIN_COLS, jnp.float32), (WIN_COLS, jnp.bfloat16), (Q_SB_COLS, jnp.bfloat16),
                (LANES, jnp.float32), (n_mg, jnp.bfloat16)]
    return pl.pallas_call(
        _proj_body,
        grid=(m // tm,),
        in_specs=[rows(d), pl.BlockSpec((1, d), lambda i: (0, 0)), pl.BlockSpec((d, n), lambda i: (0, 0))],
        out_specs=[pl.BlockSpec((NSA_HEADS, tm, LANES), lambda i: (0, i, 0))] + [rows(c) for c, _ in out_cols],
        out_shape=[jax.ShapeDtypeStruct((NSA_HEADS, m, LANES), jnp.bfloat16)]
                  + [jax.ShapeDtypeStruct((m, c), t) for c, t in out_cols],
        compiler_params=_params("parallel"),
        name="proj",
    )(x2, g, w)


def _compress_weights(cmp_w1, cmp_w2, cmp_pe):
    w1 = cmp_w1.reshape(2, 2, CMP_STRIDE, HEAD_DIM, CMP_HIDDEN)
    pair = jnp.concatenate([w1[:, 0], w1[:, 1]], axis=-1)
    gsel = jnp.eye(NSA_GROUPS, dtype=cmp_w1.dtype)
    w1s = (gsel[None, :, None, :, None, None] * pair[:, None, :, None, :, :]).reshape(
        2, NSA_GROUPS, CMP_STRIDE, LANES, 2 * CMP_HIDDEN)
    w2s = (cmp_w2[:, None, :, None, :] * gsel[None, :, None, :, None]).reshape(2, NSA_GROUPS, CMP_HIDDEN, LANES)
    pe = jnp.pad(cmp_pe.reshape(2, 1, CMP_LEN * HEAD_DIM), ((0, 0), (0, 7), (0, 0)))
    w1f = cmp_w1.reshape(2, CMP_LEN * HEAD_DIM, CMP_HIDDEN)
    return (w1s.astype(jnp.bfloat16), w2s.astype(jnp.bfloat16), pe.astype(jnp.bfloat16), w1f.astype(jnp.bfloat16))


def _compress_core(src_refs, n_chunk, w1s_ref, w2s_ref, pe_ref, w1f_ref):
    outs = []
    for z in range(2):
        const = _dot(pe_ref[z], w1f_ref[z])[0:1]
        acc = [jnp.zeros((n_chunk, 2 * CMP_HIDDEN), jnp.float32) for _ in range(NSA_GROUPS)]
        for rho in range(CMP_STRIDE):
            xr = src_refs[z][pl.ds(rho, n_chunk, stride=CMP_STRIDE), :].astype(jnp.bfloat16)
            for g in range(NSA_GROUPS):
                acc[g] = acc[g] + _dot(xr, w1s_ref[z, g, rho])
        out = jnp.zeros((n_chunk, LANES), jnp.float32)
        for g in range(NSA_GROUPS):
            nxt = pltpu.roll(acc[g][:, CMP_HIDDEN:], n_chunk - 1, 0)
            hid = _gelu_tanh(acc[g][:, :CMP_HIDDEN] + nxt + const)
            out = out + _dot(hid.astype(jnp.bfloat16), w2s_ref[z, g])
        outs.append(out)
    return outs


def _compress_body(k_ref, v_ref, w1s_ref, w2s_ref, pe_ref, w1f_ref, kc_ref, vc_ref):
    n_chunk = kc_ref.shape[0]
    kc, vc = _compress_core((k_ref, v_ref), n_chunk, w1s_ref, w2s_ref, pe_ref, w1f_ref)
    kc_ref[...] = kc.astype(kc_ref.dtype)
    vc_ref[...] = vc.astype(vc_ref.dtype)


def _full(a):
    nd = a.ndim
    return pl.BlockSpec(a.shape, lambda *_: (0,) * nd)


def _compress_prompt(kv32, cw, batch, seq):
    n_chunk = seq // CMP_STRIDE
    kv3 = kv32.reshape(batch, seq, KV_COLS)
    out = jax.ShapeDtypeStruct((batch, n_chunk, LANES), jnp.bfloat16)
    return pl.pallas_call(
        _compress_body,
        grid=(batch,),
        in_specs=[pl.BlockSpec((None, seq, LANES), lambda b: (b, 0, BLK_CMP_K)),
                  pl.BlockSpec((None, seq, LANES), lambda b: (b, 0, BLK_CMP_V))] + [_full(a) for a in cw],
        out_specs=[pl.BlockSpec((None, n_chunk, LANES), lambda b: (b, 0, 0))] * 2,
        out_shape=[out, out],
        compiler_params=_params("parallel"),
        name="compress",
    )(kv3, kv3, *cw)


def _overlap_consts(n_chunk, n_sel_lanes):
    cs = np.arange(n_chunk)[:, None] * CMP_STRIDE
    ss = np.arange(n_sel_lanes)[None, :] * SEL_LEN
    m = ((cs < ss + SEL_LEN) & (cs + CMP_LEN > ss)).astype(np.float32)
    if n_sel_lanes == LANES:
        return jnp.asarray(m[None], jnp.bfloat16)
    z = np.zeros_like(m)
    return jnp.asarray(np.stack([np.concatenate([m, z], 1), np.concatenate([z, m], 1)]), jnp.bfloat16)


def _upper_tri():
    j = np.arange(LANES)[:, None]
    s = np.arange(LANES)[None, :]
    return jnp.asarray((j > s).astype(np.float32), jnp.bfloat16)


def _flash_step(s, valid, v_tile, m_ref, l_ref, acc_ref, idx):
    s = jnp.where(valid, s, NEG_INF)
    m_old = m_ref[idx]
    m_new = jnp.maximum(m_old, jnp.max(s, axis=-1, keepdims=True))
    alpha = jnp.exp(m_old - m_new)
    p = jnp.where(valid, jnp.exp(s - m_new), 0.0)
    l_ref[idx] = alpha * l_ref[idx] + jnp.sum(p, axis=-1, keepdims=True)
    acc_ref[idx] = alpha * acc_ref[idx] + _dot(p.astype(v_tile.dtype), v_tile)
    m_ref[idx] = m_new


def _flash_init(m_ref, l_ref, acc_ref):
    m_ref[...] = jnp.full(m_ref.shape, NEG_INF, jnp.float32)
    l_ref[...] = jnp.zeros(l_ref.shape, jnp.float32)
    acc_ref[...] = jnp.zeros(acc_ref.shape, jnp.float32)


def _nsa_body(q_ref, kc_ref, vct_ref, ks_ref, vst_ref, kw_ref, vwt_ref, ng_ref, ovt_ref, o_ref,
              mem_ref, out_ref, m_ref, l_ref, acc_ref, *, tq, tk, n_cmp):
    i = pl.program_id(1)
    n_chunk = kc_ref.shape[0]
    q_pos = i * tq + _iota((1, tq), 1)
    gates = ng_ref[...].T

    nrow = _iota((n_chunk, 1), 0)
    d_cmp = q_pos - (nrow * CMP_STRIDE + (CMP_LEN - 1))
    valid_cmp = (d_cmp >= 0) & (nrow < n_cmp)
    dist_cmp = d_cmp.astype(jnp.float32)
    kc = kc_ref[...]
    vct = vct_ref[...]
    ps = jnp.zeros((LANES, tq), jnp.float32)
    for g in range(NSA_GROUPS):
        psum = jnp.zeros((n_chunk, tq), jnp.float32)
        for r in range(NSA_REP):
            h = g * NSA_REP + r
            s = _dot_nt(kc, q_ref[h]) - _SLOPES[h] * dist_cmp
            s = jnp.where(valid_cmp, s, NEG_INF)
            e = jnp.where(valid_cmp, jnp.exp(s - jnp.max(s, axis=0, keepdims=True)), 0.0)
            p = e / jnp.maximum(jnp.sum(e, axis=0, keepdims=True), 1e-30)
            psum = psum + p
            out_ref[h] = gates[3 * h:3 * h + 1, :] * _dot(vct, p.astype(vct.dtype))
        hi = psum.astype(jnp.bfloat16)
        lo = (psum - hi.astype(jnp.float32)).astype(jnp.bfloat16)
        ps = ps + _dot(ovt_ref[g], hi) + _dot(ovt_ref[g], lo)

    sub = _iota((LANES, 1), 0)
    blk = sub & (SEL_LEN - 1)
    cur = q_pos >> 6
    valid_blk = blk <= cur
    forced = (blk == 0) | (blk == cur) | (blk == cur - 1)
    score = jnp.where(valid_blk, ps + jnp.where(forced, FORCE_BONUS, 0.0), NEG_INF)
    low_half = sub < SEL_LEN
    rank = jnp.zeros((LANES, tq), jnp.float32)
    for c in range(SEL_LEN):
        other = jnp.where(low_half, score[c:c + 1, :], score[SEL_LEN + c:SEL_LEN + c + 1, :])
        beats = (other > score) | ((other == score) & (c < blk))
        rank = rank + jnp.where(beats, 1.0, 0.0)
    top_k = min(SEL_TOPK, ks_ref.shape[0] // SEL_LEN)
    unpicked = jnp.where((rank < top_k) & valid_blk, 0.0, NEG_INF).T.astype(mem_ref.dtype)
    mem_ref[...] = jnp.concatenate([unpicked] * NSA_REP, axis=0)

    wide = NSA_REP * tq
    col = _iota((1, wide), 1)
    rel = (col & (tq - 1)) - _iota((tk, 1), 0)
    rel_f = rel.astype(jnp.float32)
    causal = rel >= 0
    key_hi = jnp.where(_iota((tk, 1), 0) >= SEL_LEN, 1, 0)
    elane = _iota((tk, LANES), 1)

    def run(k_ref, vt_ref, g, kt_lo, use_sel):
        slope = jnp.zeros((1, wide), jnp.float32)
        for r in range(NSA_REP):
            slope = jnp.where((col >= r * tq) & (col < (r + 1) * tq), _SLOPES[g * NSA_REP + r], slope)
        bias = -(slope * rel_f)
        q4 = q_ref[g * NSA_REP:(g + 1) * NSA_REP].reshape(wide, LANES)
        if use_sel:
            q4 = jnp.concatenate([q4, mem_ref[...]], axis=1)
        m_ref[...] = jnp.full(m_ref.shape, NEG_INF, jnp.float32)
        l_ref[...] = jnp.zeros(l_ref.shape, jnp.float32)
        acc_ref[...] = jnp.zeros(acc_ref.shape, jnp.float32)

        def fetch(kt):
            off = pl.multiple_of(kt * tk, tk)
            k_tile = k_ref[pl.ds(off, tk), :]
            if use_sel:
                onehot = jnp.where(elane == g * SEL_LEN + kt * (tk // SEL_LEN) + key_hi, 1.0, 0.0)
                k_tile = jnp.concatenate([k_tile, onehot.astype(k_tile.dtype)], axis=1)
            return (_dot_nt(k_tile, q4),)

        def apply(pending):
            p, alpha, kt = pending
            off = pl.multiple_of(kt * tk, tk)
            acc_ref[...] = alpha * acc_ref[...] + _dot(vt_ref[:, pl.ds(off, tk)], p)

        def update(s, valid, kt):
            s = s + bias
            if valid is not None:
                s = jnp.where(valid, s, NEG_INF)
            shift = slope * ((kt - i) * tk).astype(jnp.float32)
            m_old = m_ref[...]
            m_new = jnp.maximum(m_old, jnp.max(s, axis=0, keepdims=True) + shift)
            alpha = jnp.exp(m_old - m_new)
            p = jnp.exp(s - (m_new - shift))
            l_ref[...] = alpha * l_ref[...] + jnp.sum(p, axis=0, keepdims=True)
            m_ref[...] = m_new
            return p.astype(vt_ref.dtype), alpha, kt

        def mask(kt, diag):
            if use_sel:
                return causal if diag else None
            if diag:
                return causal
            return rel < WINDOW - (i - kt) * tk

        first = fetch(i)
        pending = update(first[0], mask(i, True), i)
        n_fetch = len(first)

        def step(t, carry):
            kt = i - t
            nxt = fetch(jnp.maximum(kt - 1, 0))
            apply(carry[n_fetch:])
            pending = update(carry[0], mask(kt, False), kt)
            return nxt + pending

        out = lax.fori_loop(1, i - kt_lo + 1, step, fetch(jnp.maximum(i - 1, 0)) + pending)
        apply(out[n_fetch:])

    kt_win = jnp.maximum(i * tq - (WINDOW - 1), 0) // tk
    for g in range(NSA_GROUPS):
        for branch, (k_ref, vt_ref, lo, use_sel) in enumerate(
                [(ks_ref, vst_ref, 0, True), (kw_ref, vwt_ref, kt_win, False)]):
            run(k_ref, vt_ref, g, lo, use_sel)
            o = acc_ref[...] / jnp.maximum(l_ref[...], 1e-30)
            for r in range(NSA_REP):
                h = g * NSA_REP + r
                gate = gates[3 * h + 1 + branch:3 * h + 2 + branch, :]
                out_ref[h] = out_ref[h] + gate * o[:, r * tq:(r + 1) * tq]
    lane = _iota((tq, LANES), 1)
    for h in range(NSA_HEADS):
        keep = (lane >= SEL_LEN) == (h >= NSA_REP)
        o_ref[:, h * LANES:(h + 1) * LANES] = jnp.where(keep, out_ref[h].T, 0.0).astype(o_ref.dtype)


def _nsa_prompt(qn, kvb, winb, kc, vc, ng, batch, seq):
    tq = tk = 128
    n_chunk = seq // CMP_STRIDE
    n_cmp = (seq - CMP_LEN) // CMP_STRIDE + 1
    assert seq % tq == 0 and seq // SEL_LEN <= SEL_LEN
    ovt = jnp.swapaxes(_overlap_consts(n_chunk, SEL_LEN), 1, 2)
    kv3 = kvb.reshape(batch, seq, KV_COLS)
    win3 = winb.reshape(batch, seq, WIN_COLS)
    vct = jnp.swapaxes(vc, 1, 2)
    vst = jnp.swapaxes(kv3[:, :, BLK_SEL_V * LANES:(BLK_SEL_V + 1) * LANES], 1, 2)
    vwt = jnp.swapaxes(win3[:, :, LANES:2 * LANES], 1, 2)

    def keys_blk(c):
        return pl.BlockSpec((None, seq, LANES), lambda b, i: (b, 0, c))

    def vals_blk(n):
        return pl.BlockSpec((None, LANES, n), lambda b, i: (b, 0, 0))

    nq = seq // tq
    return pl.pallas_call(
        functools.partial(_nsa_body, tq=tq, tk=tk, n_cmp=n_cmp),
        grid=(batch, nq),
        in_specs=[pl.BlockSpec((NSA_HEADS, tq, LANES), lambda b, i: (0, b * nq + i, 0)),
                  pl.BlockSpec((None, n_chunk, LANES), lambda b, i: (b, 0, 0)), vals_blk(n_chunk),
                  keys_blk(BLK_SEL_K), vals_blk(seq),
                  keys_blk(0), vals_blk(seq),
                  pl.BlockSpec((tq, LANES), lambda b, i: (b * nq + i, 0)),
                  _full(ovt)],
        out_specs=pl.BlockSpec((tq, QN_PAD_COLS), lambda b, i: (b * nq + i, 0)),
        out_shape=jax.ShapeDtypeStruct((batch * seq, QN_PAD_COLS), jnp.bfloat16),
        scratch_shapes=[pltpu.VMEM((NSA_REP * tq, LANES), jnp.bfloat16),
                        pltpu.VMEM((NSA_HEADS, LANES, tq), jnp.float32),
                        pltpu.VMEM((1, NSA_REP * tq), jnp.float32),
                        pltpu.VMEM((1, NSA_REP * tq), jnp.float32),
                        pltpu.VMEM((LANES, NSA_REP * tq), jnp.float32)],
        compiler_params=_params("parallel", "parallel"),
        name="nsa",
    )(qn, kc, vct, kv3, vst, win3, vwt, ng, ovt)


def _softplus(z):
    return jnp.maximum(z, 0.0) + jnp.log1p(jnp.exp(-jnp.abs(z)))


EXP_UNDERFLOW = -104.0


def _sb_body(q_ref, k_ref, vt_ref, ut_ref, o_ref, *, tq, tk):
    i = pl.program_id(2)
    lane = _iota((tq, LANES), 1)
    wide = 2 * tq
    strictly_past = ((_iota((1, wide), 1) & (tq - 1)) - _iota((tk, 1), 0)) > 0
    q2 = q_ref[...].astype(jnp.float32)
    qq = jnp.concatenate([jnp.where((lane >= HEAD_DIM) == (e == 1), q2, 0.0).astype(jnp.bfloat16)
                          for e in range(2)], axis=0)
    ut = ut_ref[...]

    def fetch(kt):
        return _dot_nt(k_ref[pl.ds(pl.multiple_of(kt * tk, tk), tk), :], qq)

    def update(z, kt, rest_later, acc, past):
        sp = _softplus(z)
        log_keep = -sp if past is None else jnp.where(past, -sp, 0.0)
        hi = log_keep.astype(jnp.bfloat16)
        lo = (log_keep - hi.astype(jnp.float32)).astype(jnp.bfloat16)
        rest2 = _dot(ut, jnp.concatenate([hi, lo], axis=1))
        rest = rest2[:, :wide] + rest2[:, wide:]
        a = jnp.exp((z - sp) + rest + rest_later)
        if past is not None:
            a = jnp.where(past, a, 0.0)
        acc = acc + _dot(vt_ref[:, pl.ds(pl.multiple_of(kt * tk, tk), tk)], a.astype(vt_ref.dtype))
        return rest_later + (rest[0:1, :] + log_keep[0:1, :]), acc

    rest_later, acc = update(fetch(i), i, jnp.zeros((1, wide), jnp.float32),
                             jnp.zeros((LANES, wide), jnp.float32), strictly_past)

    def cond(c):
        return (c[0] <= i) & (c[1] > EXP_UNDERFLOW)

    def step(c):
        t, _, z, rest_later, acc = c
        kt = i - t
        z_next = fetch(jnp.maximum(kt - 1, 0))
        rest_later, acc = update(z, kt, rest_later, acc, None)
        return t + 1, jnp.max(rest_later), z_next, rest_later, acc

    out = lax.while_loop(cond, step, (jnp.int32(1), jnp.max(rest_later), fetch(jnp.maximum(i - 1, 0)),
                                      rest_later, acc))
    acc = out[4]
    sub = _iota((LANES, 1), 0)
    o_ref[...] = jnp.where(sub < HEAD_DIM, acc[:, :tq], acc[:, tq:]).T.astype(o_ref.dtype)


def _sb_prompt(qs, kvb, batch, seq):
    tq = tk = 128
    assert seq % tq == 0
    kv3 = kvb.reshape(batch, seq, KV_COLS)
    nq = seq // tq
    n_pair = SB_HEADS // 2
    ut = _upper_tri().T
    vt = jnp.swapaxes(kv3[:, :, BLK_SB_V * LANES:], 1, 2)
    return pl.pallas_call(
        functools.partial(_sb_body, tq=tq, tk=tk),
        grid=(batch, n_pair, nq),
        in_specs=[pl.BlockSpec((tq, LANES), lambda b, p, i: (b * nq + i, p)),
                  pl.BlockSpec((None, seq, LANES), lambda b, p, i: (b, 0, BLK_SB_K + p)),
                  pl.BlockSpec((None, LANES, seq), lambda b, p, i: (b, p, 0)),
                  _full(ut)],
        out_specs=pl.BlockSpec((tq, LANES), lambda b, p, i: (b * nq + i, p)),
        out_shape=jax.ShapeDtypeStruct((batch * seq, Q_SB_COLS), jnp.bfloat16),
        compiler_params=_params("parallel", "parallel", "parallel"),
        name="sb",
    )(qs, kv3, vt, ut)


def _sample_a_body(pt_ref, *refs, page, n_pages, pps):
    del pt_ref
    pool_refs = refs[:pps]
    q_ref, w1s_ref, w2s_ref, pe_ref, w1f_ref, ov_ref, ocmp_ref, mem_ref, past_ref = refs[pps:]
    j = pl.program_id(1)
    for k in range(pps):
        rows = pl.ds(pl.multiple_of((j * pps + k) * page, page), page)
        for z in range(2):
            for g in range(NSA_GROUPS):
                past_ref[z, rows, g * HEAD_DIM:(g + 1) * HEAD_DIM] = pool_refs[k][:, z * NSA_GROUPS + g, :]

    @pl.when(j == n_pages // pps - 1)
    def _():
        past_len = n_pages * page
        n_chunk = past_len // CMP_STRIDE
        n_cmp = (past_len + 1 - CMP_LEN) // CMP_STRIDE + 1
        n_past_blk = past_len // SEL_LEN
        kc, vc = _compress_core((past_ref.at[0], past_ref.at[1]), n_chunk, w1s_ref, w2s_ref, pe_ref, w1f_ref)
        kc = kc.astype(jnp.bfloat16)
        vc = vc.astype(jnp.bfloat16)
        q8 = q_ref[...]
        ncol = _iota((NSA_HEADS, n_chunk), 1)
        d = past_len - (ncol * CMP_STRIDE + (CMP_LEN - 1))
        valid = (d >= 0) & (ncol < n_cmp)
        hrow = _iota((NSA_HEADS, 1), 0)
        s = _dot_nt(q8, kc) - _slope_col() * d.astype(jnp.float32)
        s = jnp.where(valid, s, NEG_INF)
        e = jnp.where(valid, jnp.exp(s - jnp.max(s, axis=-1, keepdims=True)), 0.0)
        p = e / jnp.maximum(jnp.sum(e, axis=-1, keepdims=True), 1e-30)
        ocmp_ref[...] = _dot(p.astype(vc.dtype), vc)
        psum = jnp.zeros_like(p)
        for g in range(NSA_GROUPS):
            pg = jnp.sum(p[g * NSA_REP:(g + 1) * NSA_REP], axis=0, keepdims=True)
            psum = jnp.where((hrow >= g * NSA_REP) & (hrow < (g + 1) * NSA_REP), pg, psum)
        ps = _dot_split(psum, ov_ref[0])
        blk = _iota((NSA_HEADS, LANES), 1)
        forced = (blk == 0) | (blk == n_past_blk - 1)
        score = jnp.where(blk < n_past_blk, ps + jnp.where(forced, FORCE_BONUS, 0.0), NEG_INF)
        eye = _iota((LANES, LANES), 0) == _iota((LANES, LANES), 1)
        before = _iota((LANES, LANES), 0) < _iota((LANES, LANES), 1)
        for g in range(NSA_GROUPS):
            srow = score[g * NSA_REP:g * NSA_REP + 1, :]
            scol = jnp.sum(jnp.where(eye, srow, 0.0), axis=1, keepdims=True)
            beats = (scol > srow) | ((scol == srow) & before)
            rank = jnp.sum(jnp.where(beats, 1.0, 0.0), axis=0, keepdims=True)
            top_k = min(SEL_TOPK, n_past_blk + 1)
            member = jnp.where((rank < top_k - 1) & (blk[0:1] < n_past_blk), 1.0, 0.0)
            mem_ref[g * NSA_REP:(g + 1) * NSA_REP, :] = jnp.broadcast_to(member, (NSA_REP, LANES))


def _pages_per_step(n_pages, want):
    while n_pages % want:
        want //= 2
    return want


def _sample_a(pool4, pt_flat, q8, cw, dec, n_pages, page):
    past_len = n_pages * page
    n_chunk = past_len // CMP_STRIDE
    assert past_len // SEL_LEN <= LANES and past_len % SEL_LEN == 0
    ov = _overlap_consts(n_chunk, LANES)
    o8 = jax.ShapeDtypeStruct((dec, NSA_HEADS, LANES), jnp.float32)
    pps = _pages_per_step(n_pages, 8)

    def const(a):
        nd = a.ndim
        return pl.BlockSpec(a.shape, lambda b, j, pt: (0,) * nd)

    def page_spec(k):
        return pl.BlockSpec((None, page, 8, HEAD_DIM), lambda b, j, pt: (pt[b * n_pages + j * pps + k], 0, 0, 0))

    grid_spec = pltpu.PrefetchScalarGridSpec(
        num_scalar_prefetch=1,
        grid=(dec, n_pages // pps),
        in_specs=[page_spec(k) for k in range(pps)]
                 + [pl.BlockSpec((None, NSA_HEADS, LANES), lambda b, j, pt: (b, 0, 0))]
                 + [const(a) for a in cw] + [const(ov)],
        out_specs=[pl.BlockSpec((None, NSA_HEADS, LANES), lambda b, j, pt: (b, 0, 0))] * 2,
        scratch_shapes=[pltpu.VMEM((2, past_len, LANES), jnp.float32)],
    )
    return pl.pallas_call(
        functools.partial(_sample_a_body, page=page, n_pages=n_pages, pps=pps),
        grid_spec=grid_spec,
        out_shape=[o8, o8],
        compiler_params=_params("parallel", "arbitrary"),
        name="sample_a",
    )(pt_flat, *([pool4] * pps), q8, *cw, ov)


def _sample_b_body(pt_ref, *refs, page, n_pages, wbuf, pps):
    del pt_ref
    pool_refs = refs[:pps]
    (qn_ref, qs_ref, kvn_ref, winn_ref, cw_ref, mem_ref, ocmp_ref, gate_ref, u_ref,
     oa_ref, ob_ref, wout_ref, m_ref, l_ref, acc_ref, rest_ref, sbacc_ref, dense_ref) = refs[pps:]
    jj = pl.program_id(1)
    n_steps = n_pages // pps
    past_len = n_pages * page
    q8 = qn_ref[...]
    hrow = _iota((NSA_HEADS, 1), 0)
    slope = _slope_col()
    lane8 = _iota((NSA_HEADS, LANES), 1)
    own_half = (lane8 >= HEAD_DIM) == (hrow >= NSA_REP)

    def new_row(ref, c):
        return ref[:, c * LANES:(c + 1) * LANES].astype(jnp.bfloat16).astype(jnp.float32)

    @pl.when(jj == 0)
    def _():
        qf = q8.astype(jnp.float32)
        m_ref[0] = jnp.sum(qf * new_row(kvn_ref, BLK_SEL_K), axis=-1, keepdims=True)
        l_ref[0] = jnp.ones((NSA_HEADS, 1), jnp.float32)
        acc_ref[0] = jnp.broadcast_to(new_row(kvn_ref, BLK_SEL_V), (NSA_HEADS, LANES))
        rest_ref[...] = jnp.zeros(rest_ref.shape, jnp.float32)
        sbacc_ref[...] = jnp.zeros(sbacc_ref.shape, jnp.float32)

    qs = qs_ref[...]
    lane_sb = _iota((SB_HEADS, Q_SB_COLS), 1)
    head_sb = _iota((SB_HEADS, Q_SB_COLS), 0)
    own_sb = (lane_sb >> 6) == head_sb
    qbd = jnp.where(own_sb, qs.astype(jnp.float32), 0.0).astype(jnp.bfloat16)
    member = mem_ref[...].astype(jnp.bfloat16)
    keys = pps * page
    first_sel = BLK_SEL_K * LANES // HEAD_DIM
    first_sb = BLK_SB_K * LANES // HEAD_DIM

    def regroup(row_lo, row_hi):
        for k in range(pps):
            for r in range(row_lo, row_hi):
                dense_ref[k, :, r * HEAD_DIM:(r + 1) * HEAD_DIM] = pool_refs[k][pl.ds(r, page, stride=KV_ROWS), :]

    def dense(c0, c1):
        return dense_ref[:, :, c0:c1].reshape(keys, c1 - c0).astype(jnp.bfloat16)

    regroup(first_sel, first_sb)
    lane_k = _iota((1, keys), 1)
    tok = lane_k & (page - 1)
    page_of = n_pages - 1 - (jj * pps + (lane_k >> 7))
    d = past_len - (page_of * page + tok)
    expand = jnp.where(_iota((LANES, keys), 0) == page_of * (page // SEL_LEN) + (tok >> 6), 1.0, 0.0)
    picked = _dot(member, expand.astype(jnp.bfloat16))
    s = _dot_nt(q8, dense(BLK_SEL_K * LANES, BLK_SEL_V * LANES)) - slope * d.astype(jnp.float32)
    _flash_step(s, picked > 0.5, dense(BLK_SEL_V * LANES, BLK_SB_K * LANES), m_ref, l_ref, acc_ref, 0)

    @pl.when(jnp.max(rest_ref[...]) > EXP_UNDERFLOW)
    def _():
        regroup(first_sb, KV_ROWS)
        z = _dot_nt(qbd, dense(BLK_SB_K * LANES, BLK_SB_V * LANES))
        sp = _softplus(z)
        log_keep = -sp
        rest = _dot_split(log_keep, u_ref[...])
        a = jnp.exp((z - sp) + rest + rest_ref[...])
        sbacc_ref[...] = sbacc_ref[...] + _dot(a.astype(jnp.bfloat16), dense(BLK_SB_V * LANES, KV_COLS))
        oldest = (pps - 1) * page
        rest_ref[...] = rest_ref[...] + (rest[:, oldest:oldest + 1] + log_keep[:, oldest:oldest + 1])

    @pl.when(jj == n_steps - 1)
    def _():
        o_sel = acc_ref[0] / jnp.maximum(l_ref[0], 1e-30)
        cw = cw_ref[...]
        k_w = cw[:, 0:LANES].astype(jnp.bfloat16)
        v_w = cw[:, LANES:2 * LANES].astype(jnp.bfloat16)
        wcol = _iota((NSA_HEADS, wbuf), 1)
        dw = wbuf - wcol
        valid_w = (dw < WINDOW) & (past_len - dw >= 0)
        s_w = jnp.where(valid_w, _dot_nt(q8, k_w) - slope * dw.astype(jnp.float32), NEG_INF)
        s_n = jnp.sum(q8.astype(jnp.float32) * new_row(winn_ref, 0), axis=-1, keepdims=True)
        m_w = jnp.maximum(jnp.max(s_w, axis=-1, keepdims=True), s_n)
        p_w = jnp.where(valid_w, jnp.exp(s_w - m_w), 0.0)
        p_n = jnp.exp(s_n - m_w)
        l_w = jnp.sum(p_w, axis=-1, keepdims=True) + p_n
        o_win = (_dot(p_w.astype(jnp.bfloat16), v_w)
                 + p_n.astype(jnp.bfloat16).astype(jnp.float32) * new_row(winn_ref, 1)) / jnp.maximum(l_w, 1e-30)
        gate = gate_ref[...]
        o_a = gate[:, 0:1] * ocmp_ref[...] + gate[:, 1:2] * o_sel + gate[:, 2:3] * o_win
        oa_ref[...] = jnp.where(own_half, o_a, 0.0)
        ob_ref[...] = jnp.sum(jnp.where(own_sb, sbacc_ref[...], 0.0), axis=0, keepdims=True)
        wrow = _iota((wbuf, 1), 0)
        wout_ref[...] = jnp.where(wrow == wbuf - 1, winn_ref[...], pltpu.roll(cw, wbuf - 1, 0))


def _sample_b(pool_rows, pt_flat, q8, qs, kv32, win32, cache_win2, member, ocmp, gate, dec, n_pages, page):
    wbuf = cache_win2.shape[1]
    assert page == LANES
    pps = _pages_per_step(n_pages, 8)
    slot, tok = np.divmod(np.arange(pps * page), page)
    later = (slot[:, None] < slot[None, :]) | ((slot[:, None] == slot[None, :]) & (tok[:, None] > tok[None, :]))
    u = jnp.asarray(later.astype(np.float32), jnp.bfloat16)

    def per_seq(*shape):
        nd = len(shape)
        return pl.BlockSpec((None,) + shape, lambda b, jj, pt: (b,) + (0,) * nd)

    def page_spec(k):
        return pl.BlockSpec((None, page * KV_ROWS, HEAD_DIM),
                            lambda b, jj, pt: (pt[b * n_pages + n_pages - 1 - (jj * pps + k)], 0, 0))

    grid_spec = pltpu.PrefetchScalarGridSpec(
        num_scalar_prefetch=1,
        grid=(dec, n_pages // pps),
        in_specs=[page_spec(k) for k in range(pps)] + [
                  per_seq(NSA_HEADS, LANES), per_seq(1, Q_SB_COLS), per_seq(1, KV_COLS), per_seq(1, WIN_COLS),
                  per_seq(wbuf, WIN_COLS), per_seq(NSA_HEADS, LANES), per_seq(NSA_HEADS, LANES),
                  per_seq(NSA_HEADS, 3),
                  pl.BlockSpec(u.shape, lambda b, jj, pt: (0, 0))],
        out_specs=[per_seq(NSA_HEADS, LANES), per_seq(1, Q_SB_COLS), per_seq(wbuf, WIN_COLS)],
        scratch_shapes=[pltpu.VMEM((1, NSA_HEADS, 1), jnp.float32), pltpu.VMEM((1, NSA_HEADS, 1), jnp.float32),
                        pltpu.VMEM((1, NSA_HEADS, LANES), jnp.float32), pltpu.VMEM((SB_HEADS, 1), jnp.float32),
                        pltpu.VMEM((SB_HEADS, Q_SB_COLS), jnp.float32),
                        pltpu.VMEM((pps, page, KV_COLS), jnp.float32)],
    )
    return pl.pallas_call(
        functools.partial(_sample_b_body, page=page, n_pages=n_pages, wbuf=wbuf, pps=pps),
        grid_spec=grid_spec,
        out_shape=[jax.ShapeDtypeStruct((dec, NSA_HEADS, LANES), jnp.float32),
                   jax.ShapeDtypeStruct((dec, 1, Q_SB_COLS), jnp.float32),
                   jax.ShapeDtypeStruct((dec, wbuf, WIN_COLS), jnp.float32)],
        compiler_params=_params("parallel", "arbitrary"),
        name="sample_b",
    )(pt_flat, *([pool_rows] * pps), q8, qs, kv32, win32, cache_win2, member, ocmp, gate, u)


def _post_body(x_ref, oa_ref, ob_ref, mg_ref, wa_ref, wb_ref, wo_ref, g_ref, y_ref):
    d = x_ref.shape[1]
    u_a = _dot(oa_ref[...].astype(jnp.bfloat16), wa_ref[...])
    u_b = _dot(ob_ref[...].astype(jnp.bfloat16), wb_ref[...])
    mixed = mg_ref[:, 0:d].astype(jnp.float32) * u_a + mg_ref[:, d:2 * d].astype(jnp.float32) * u_b
    y_ref[...] = x_ref[...] + _rms(_dot(mixed.astype(jnp.bfloat16), wo_ref[...]), g_ref[...])


def _post(x2, oa, ob, mg, wa, wb, wo, g):
    m, d = x2.shape
    tm = min(256, m)

    def rows(c):
        return pl.BlockSpec((tm, c), lambda i: (i, 0))

    return pl.pallas_call(
        _post_body,
        grid=(m // tm,),
        in_specs=[rows(d), rows(oa.shape[1]), rows(ob.shape[1]), rows(mg.shape[1]),
                  _full(wa), _full(wb), _full(wo), _full(g)],
        out_specs=rows(d),
        out_shape=jax.ShapeDtypeStruct((m, d), jnp.float32),
        compiler_params=_params("parallel"),
        name="post",
    )(x2, oa, ob, mg, wa, wb, wo, g)


def _ffn_body(x_ref, prev_ref, g2_ref, g3_ref, wup_ref, cw_ref, cb_ref, wdn_ref, y_ref, st_ref, carry_ref,
              *, tm, d_ff, fc, per_row):
    t = pl.program_id(1)
    x = x_ref[...]
    h = _rms(x, g2_ref[...]).astype(jnp.bfloat16)
    if not per_row:
        @pl.when(t == 0)
        def _():
            carry_ref[...] = prev_ref[...]
    row = _iota((tm, 1), 0)
    acc = jnp.zeros((tm, x.shape[1]), jnp.float32)
    for c0 in range(0, d_ff, fc):
        halves = []
        for base in (c0, d_ff + c0):
            cols = slice(base, base + fc)
            u = _dot(h, wup_ref[:, cols])
            if per_row:
                u1 = prev_ref[1, :, cols]
                u2 = prev_ref[0, :, cols]
                st_ref[0, :, cols] = u1
                st_ref[1, :, cols] = u
            else:
                p1 = carry_ref[1:2, cols]
                p2 = carry_ref[0:1, cols]
                u1 = jnp.where(row < 1, p1, pltpu.roll(u, 1, 0))
                u2 = jnp.where(row < 1, p2, jnp.where(row < 2, p1, pltpu.roll(u, 2, 0)))
                carry_ref[:, cols] = u[tm - 2:tm, :]
                st_ref[:, cols] = u[tm - 2:tm, :]
            halves.append(cb_ref[:, cols] + u2 * cw_ref[0:1, cols] + u1 * cw_ref[1:2, cols] + u * cw_ref[2:3, cols])
        act = (_gelu_tanh(halves[0]) * halves[1]).astype(jnp.bfloat16)
        acc = acc + _dot(act, wdn_ref[c0:c0 + fc, :])
    y_ref[...] = x + _rms(acc, g3_ref[...])


def _ffn(x2, prev, g2, g3, wup, cw, cb, wdn, batch, seq, per_row):
    m, d = x2.shape
    d_ff = wdn.shape[0]
    fc = 256
    assert d_ff % fc == 0
    if per_row:
        tm, nb, nt = m, 1, 1
        prev_spec = pl.BlockSpec((2, tm, 2 * d_ff), lambda b, t: (0, 0, 0))
        st_spec = pl.BlockSpec((2, tm, 2 * d_ff), lambda b, t: (0, 0, 0))
        st_shape = jax.ShapeDtypeStruct((2, m, 2 * d_ff), jnp.float32)
    else:
        tm = min(256, seq)
        nb, nt = batch, seq // tm
        prev_spec = pl.BlockSpec((None, 2, 2 * d_ff), lambda b, t: (b, 0, 0))
        st_spec = pl.BlockSpec((None, 2, 2 * d_ff), lambda b, t: (b, 0, 0))
        st_shape = jax.ShapeDtypeStruct((batch, 2, 2 * d_ff), jnp.float32)
    return pl.pallas_call(
        functools.partial(_ffn_body, tm=tm, d_ff=d_ff, fc=fc, per_row=per_row),
        grid=(nb, nt),
        in_specs=[pl.BlockSpec((tm, d), lambda b, t: (b * nt + t, 0)), prev_spec,
                  _full(g2), _full(g3), _full(wup), _full(cw), _full(cb), _full(wdn)],
        out_specs=[pl.BlockSpec((tm, d), lambda b, t: (b * nt + t, 0)), st_spec],
        out_shape=[jax.ShapeDtypeStruct((m, d), jnp.float32), st_shape],
        scratch_shapes=[pltpu.VMEM((2, 2 * d_ff), jnp.float32)],
        compiler_params=_params("parallel", "arbitrary"),
        name="ffn",
    )(x2, prev, g2, g3, wup, cw, cb, wdn)


def _branch_a_weights(w_branch_a):
    d = w_branch_a.shape[1]
    w = w_branch_a.reshape(NSA_GROUPS, NSA_REP, 1, HEAD_DIM, d)
    half = jnp.eye(NSA_GROUPS, dtype=w.dtype)
    return (w * half[:, None, :, None, None]).reshape(QN_PAD_COLS, d).astype(jnp.bfloat16)


def kernel(x_prompt, x_sample, cache_kv, cache_win, state_conv, page_table, norm_g, w_in, cmp_w1, cmp_w2, cmp_pe,
           w_branch_a, w_branch_b, w_out, ffn_w_up, ffn_conv_w, ffn_conv_b, ffn_w_down):
    depth = norm_g.shape[0]
    assert depth == 1, "one layer per call: outputs of deeper stacks are not wired"
    batch, seq, d_model = x_prompt.shape
    dec, dec_seq, _ = x_sample.shape
    assert dec_seq == 1
    n_pool, page = cache_kv.shape[1], cache_kv.shape[2]
    n_pages = page_table.shape[1]
    wbuf = cache_win.shape[2]
    d_ff = ffn_w_down.shape[1]
    l = 0

    g = norm_g[l].reshape(4, 1, d_model)
    wp = _proj_weights(w_in[l], d_model)
    cw = _compress_weights(cmp_w1[l], cmp_w2[l], cmp_pe[l])
    wa = _branch_a_weights(w_branch_a[l])
    wb = w_branch_b[l].astype(jnp.bfloat16)
    wo = w_out[l].astype(jnp.bfloat16)
    wup = ffn_w_up[l].astype(jnp.bfloat16)
    wdn = ffn_w_down[l].astype(jnp.bfloat16)
    conv_w = ffn_conv_w[l]
    conv_b = ffn_conv_b[l].reshape(1, 2 * d_ff)

    xp = x_prompt.reshape(batch * seq, d_model)
    qn, kv32, kvb, win32, winb, qs, ng, mg = _proj(xp, g[0], wp)
    kc, vc = _compress_prompt(kv32, cw, batch, seq)
    oa = _nsa_prompt(qn, kvb, winb, kc, vc, ng, batch, seq)
    ob = _sb_prompt(qs, kvb, batch, seq)
    x1 = _post(xp, oa, ob, mg, wa, wb, wo, g[1])
    conv0 = jnp.zeros((batch, CONV_W - 1, 2 * d_ff), jnp.float32)
    yp, conv_p = _ffn(x1, conv0, g[2], g[3], wup, conv_w, conv_b, wdn, batch, seq, per_row=False)
    kv_p = kv32.reshape(1, batch, seq, KV_ROWS, HEAD_DIM)
    win_seq = win32.reshape(batch, seq, WIN_ROWS, HEAD_DIM)
    win_p = jnp.pad(win_seq, ((0, 0), (wbuf, 0), (0, 0), (0, 0)))[:, -wbuf:][None]

    xs = x_sample.reshape(dec, d_model)
    qn_s, kv32_s, _, win32_s, _, qs_s, ng_s, mg_s = _proj(xs, g[0], wp)
    pool4 = cache_kv[l]
    pool_rows = pool4.reshape(n_pool, page * KV_ROWS, HEAD_DIM)
    pt_flat = page_table.reshape(dec * n_pages)
    q8 = jnp.swapaxes(qn_s, 0, 1)
    ocmp, member = _sample_a(pool4, pt_flat, q8, cw, dec, n_pages, page)
    gate = ng_s[:, :NSA_GATE_COLS].reshape(dec, NSA_HEADS, 3)
    oa_s, ob_s, win_s = _sample_b(pool_rows, pt_flat, q8, qs_s.reshape(dec, 1, Q_SB_COLS),
                                  kv32_s.reshape(dec, 1, KV_COLS), win32_s.reshape(dec, 1, WIN_COLS),
                                  cache_win[l].reshape(dec, wbuf, WIN_COLS), member, ocmp, gate, dec, n_pages, page)
    x1_s = _post(xs, oa_s.reshape(dec, QN_PAD_COLS), ob_s.reshape(dec, Q_SB_COLS), mg_s, wa, wb, wo, g[1])
    prev_s = jnp.swapaxes(state_conv[l], 0, 1)
    ys, conv_s = _ffn(x1_s, prev_s, g[2], g[3], wup, conv_w, conv_b, wdn, dec, 1, per_row=True)

    return (yp.reshape(batch, seq, d_model), ys.reshape(dec, 1, d_model), kv_p,
            kv32_s.reshape(1, dec, 1, KV_ROWS, HEAD_DIM), win_p,
            win_s.reshape(1, dec, wbuf, WIN_ROWS, HEAD_DIM), conv_p[None],
            jnp.swapaxes(conv_s, 0, 1)[None])
```

```python
import functools
import math

import numpy as np
import jax
import jax.numpy as jnp
from jax import lax
from jax.experimental import pallas as pl
from jax.experimental.pallas import tpu as pltpu

HEAD_DIM = 64
NSA_HEADS = 8
NSA_GROUPS = 2
NSA_REP = NSA_HEADS // NSA_GROUPS
SB_HEADS = 8
CMP_LEN = 32
CMP_STRIDE = 16
CMP_HIDDEN = 128
SEL_LEN = 64
SEL_TOPK = 16
WINDOW = 512
CONV_W = 3
RMS_EPS = 1e-6
NEG_INF = -1e30
FORCE_BONUS = 1e6
ATTN_SCALE = HEAD_DIM ** -0.5
KV_ROWS = 4 * NSA_GROUPS + 2 * SB_HEADS
WIN_ROWS = 2 * NSA_GROUPS

LANES = 128
Q_NSA_COLS = NSA_HEADS * HEAD_DIM
KV_COLS = KV_ROWS * HEAD_DIM
WIN_COLS = WIN_ROWS * HEAD_DIM
Q_SB_COLS = SB_HEADS * HEAD_DIM
NSA_GATE_COLS = 3 * NSA_HEADS
QN_PAD_COLS = NSA_HEADS * LANES

BLK_CMP_K, BLK_CMP_V, BLK_SEL_K, BLK_SEL_V = 0, 1, 2, 3
BLK_SB_K = 4
BLK_SB_V = 4 + SB_HEADS // 2

VMEM_LIMIT = 56 * 1024 * 1024

_SLOPES = [float(np.power(np.float32(2.0), np.float32(-8.0 * (h + 1) / NSA_HEADS))) for h in range(NSA_HEADS)]


def _dot(a, b):
    return jnp.dot(a, b, preferred_element_type=jnp.float32)


def _dot_nt(a, b):
    return lax.dot_general(a, b, (((1,), (1,)), ((), ())), preferred_element_type=jnp.float32)


def _dot_split(x, w):
    hi = x.astype(jnp.bfloat16)
    lo = (x - hi.astype(jnp.float32)).astype(jnp.bfloat16)
    return _dot(hi, w) + _dot(lo, w)


def _rms(x, g):
    return x * lax.rsqrt(jnp.mean(x * x, axis=-1, keepdims=True) + RMS_EPS) * g


def _gelu_tanh(x):
    return 0.5 * x * (1.0 + jnp.tanh(math.sqrt(2.0 / math.pi) * (x + 0.044715 * (x * x * x))))


def _sigmoid(x):
    return 1.0 / (1.0 + jnp.exp(-x))


def _iota(shape, dim):
    return lax.broadcasted_iota(jnp.int32, shape, dim)


def _slope_col():
    hrow = _iota((NSA_HEADS, 1), 0)
    out = jnp.zeros((NSA_HEADS, 1), jnp.float32)
    for h in range(NSA_HEADS):
        out = jnp.where(hrow == h, _SLOPES[h], out)
    return out


def _params(*sem):
    return pltpu.CompilerParams(dimension_semantics=sem, vmem_limit_bytes=VMEM_LIMIT)


_SEG_QN = (0, QN_PAD_COLS)
_SEG_KV = (_SEG_QN[1], _SEG_QN[1] + KV_COLS)
_SEG_WIN = (_SEG_KV[1], _SEG_KV[1] + WIN_COLS)
_SEG_QS = (_SEG_WIN[1], _SEG_WIN[1] + Q_SB_COLS)
_SEG_NG = (_SEG_QS[1], _SEG_QS[1] + LANES)
PROJ_PAD_COLS_FIXED = _SEG_NG[1]


def _proj_weights(w_in, d_model):
    o1 = Q_NSA_COLS
    o2 = o1 + KV_COLS
    o3 = o2 + WIN_COLS
    o4 = o3 + Q_SB_COLS
    o5 = o4 + NSA_GATE_COLS
    wq = (w_in[:, :o1] * ATTN_SCALE).reshape(d_model, NSA_GROUPS, NSA_REP, 1, HEAD_DIM)
    half = (jnp.arange(NSA_GROUPS)[:, None] == jnp.arange(2)[None, :]).astype(w_in.dtype)
    wq = (wq * half[None, :, None, :, None]).reshape(d_model, QN_PAD_COLS)
    ng = jnp.pad(w_in[:, o4:o5], ((0, 0), (0, LANES - NSA_GATE_COLS)))
    w = jnp.concatenate([wq, w_in[:, o1:o3], w_in[:, o3:o4] * ATTN_SCALE, ng, w_in[:, o5:]], axis=1)
    return w.astype(jnp.bfloat16)


def _proj_body(x_ref, g_ref, w_ref, qn_ref, kv_ref, kvb_ref, win_ref, winb_ref, qs_ref, ng_ref, mg_ref):
    h = _rms(x_ref[...], g_ref[...]).astype(jnp.bfloat16)

    def seg(c0, c1, fn, step=512):
        for a in range(c0, c1, step):
            b = min(a + step, c1)
            fn(a - c0, b - c0, _dot(h, w_ref[:, a:b]))

    def put_qn(a, b, r):
        for c in range(a, b, LANES):
            qn_ref[c // LANES] = r[:, c - a:c - a + LANES].astype(qn_ref.dtype)

    def put_kv(a, b, r):
        kv_ref[:, a:b] = r
        kvb_ref[:, a:b] = r.astype(kvb_ref.dtype)

    def put_win(a, b, r):
        win_ref[:, a:b] = r
        winb_ref[:, a:b] = r.astype(winb_ref.dtype)

    def put_qs(a, b, r):
        qs_ref[:, a:b] = r.astype(qs_ref.dtype)

    def put_ng(a, b, r):
        ng_ref[:, a:b] = _sigmoid(r)

    def put_mg(a, b, r):
        mg_ref[:, a:b] = _sigmoid(r).astype(mg_ref.dtype)

    seg(*_SEG_QN, put_qn)
    seg(*_SEG_KV, put_kv)
    seg(*_SEG_WIN, put_win)
    seg(*_SEG_QS, put_qs)
    seg(*_SEG_NG, put_ng)
    seg(PROJ_PAD_COLS_FIXED, w_ref.shape[1], put_mg)


def _proj(x2, g, w):
    m, d = x2.shape
    tm = min(256, m)
    n = w.shape[1]
    n_mg = n - PROJ_PAD_COLS_FIXED

    def rows(c):
        return pl.BlockSpec((tm, c), lambda i: (i, 0))

    out_cols = [(KV_COLS, jnp.float32), (KV_COLS, jnp.bfloat16),
                (WIN_COLS, jnp.float32), (WIN_COLS, jnp.bfloat16), (Q_SB_COLS, jnp.bfloat16),
                (LANES, jnp.float32), (n_mg, jnp.bfloat16)]
    return pl.pallas_call(
        _proj_body,
        grid=(m // tm,),
        in_specs=[rows(d), pl.BlockSpec((1, d), lambda i: (0, 0)), pl.BlockSpec((d, n), lambda i: (0, 0))],
        out_specs=[pl.BlockSpec((NSA_HEADS, tm, LANES), lambda i: (0, i, 0))] + [rows(c) for c, _ in out_cols],
        out_shape=[jax.ShapeDtypeStruct((NSA_HEADS, m, LANES), jnp.bfloat16)]
                  + [jax.ShapeDtypeStruct((m, c), t) for c, t in out_cols],
        compiler_params=_params("parallel"),
        name="proj",
    )(x2, g, w)


def _compress_weights(cmp_w1, cmp_w2, cmp_pe):
    w1 = cmp_w1.reshape(2, 2, CMP_STRIDE, HEAD_DIM, CMP_HIDDEN)
    pair = jnp.concatenate([w1[:, 0], w1[:, 1]], axis=-1)
    gsel = jnp.eye(NSA_GROUPS, dtype=cmp_w1.dtype)
    w1s = (gsel[None, :, None, :, None, None] * pair[:, None, :, None, :, :]).reshape(
        2, NSA_GROUPS, CMP_STRIDE, LANES, 2 * CMP_HIDDEN)
    w2s = (cmp_w2[:, None, :, None, :] * gsel[None, :, None, :, None]).reshape(2, NSA_GROUPS, CMP_HIDDEN, LANES)
    pe = jnp.pad(cmp_pe.reshape(2, 1, CMP_LEN * HEAD_DIM), ((0, 0), (0, 7), (0, 0)))
    w1f = cmp_w1.reshape(2, CMP_LEN * HEAD_DIM, CMP_HIDDEN)
    return (w1s.astype(jnp.bfloat16), w2s.astype(jnp.bfloat16), pe.astype(jnp.bfloat16), w1f.astype(jnp.bfloat16))


def _compress_core(src_refs, n_chunk, w1s_ref, w2s_ref, pe_ref, w1f_ref):
    outs = []
    for z in range(2):
        const = _dot(pe_ref[z], w1f_ref[z])[0:1]
        acc = [jnp.zeros((n_chunk, 2 * CMP_HIDDEN), jnp.float32) for _ in range(NSA_GROUPS)]
        for rho in range(CMP_STRIDE):
            xr = src_refs[z][pl.ds(rho, n_chunk, stride=CMP_STRIDE), :].astype(jnp.bfloat16)
            for g in range(NSA_GROUPS):
                acc[g] = acc[g] + _dot(xr, w1s_ref[z, g, rho])
        out = jnp.zeros((n_chunk, LANES), jnp.float32)
        for g in range(NSA_GROUPS):
            nxt = pltpu.roll(acc[g][:, CMP_HIDDEN:], n_chunk - 1, 0)
            hid = _gelu_tanh(acc[g][:, :CMP_HIDDEN] + nxt + const)
            out = out + _dot(hid.astype(jnp.bfloat16), w2s_ref[z, g])
        outs.append(out)
    return outs


def _compress_body(k_ref, v_ref, w1s_ref, w2s_ref, pe_ref, w1f_ref, kc_ref, vc_ref):
    n_chunk = kc_ref.shape[0]
    kc, vc = _compress_core((k_ref, v_ref), n_chunk, w1s_ref, w2s_ref, pe_ref, w1f_ref)
    kc_ref[...] = kc.astype(kc_ref.dtype)
    vc_ref[...] = vc.astype(vc_ref.dtype)


def _full(a):
    nd = a.ndim
    return pl.BlockSpec(a.shape, lambda *_: (0,) * nd)


def _compress_prompt(kv32, cw, batch, seq):
    n_chunk = seq // CMP_STRIDE
    kv3 = kv32.reshape(batch, seq, KV_COLS)
    out = jax.ShapeDtypeStruct((batch, n_chunk, LANES), jnp.bfloat16)
    return pl.pallas_call(
        _compress_body,
        grid=(batch,),
        in_specs=[pl.BlockSpec((None, seq, LANES), lambda b: (b, 0, BLK_CMP_K)),
                  pl.BlockSpec((None, seq, LANES), lambda b: (b, 0, BLK_CMP_V))] + [_full(a) for a in cw],
        out_specs=[pl.BlockSpec((None, n_chunk, LANES), lambda b: (b, 0, 0))] * 2,
        out_shape=[out, out],
        compiler_params=_params("parallel"),
        name="compress",
    )(kv3, kv3, *cw)


def _overlap_consts(n_chunk, n_sel_lanes):
    cs = np.arange(n_chunk)[:, None] * CMP_STRIDE
    ss = np.arange(n_sel_lanes)[None, :] * SEL_LEN
    m = ((cs < ss + SEL_LEN) & (cs + CMP_LEN > ss)).astype(np.float32)
    if n_sel_lanes == LANES:
        return jnp.asarray(m[None], jnp.bfloat16)
    z = np.zeros_like(m)
    return jnp.asarray(np.stack([np.concatenate([m, z], 1), np.concatenate([z, m], 1)]), jnp.bfloat16)


def _upper_tri():
    j = np.arange(LANES)[:, None]
    s = np.arange(LANES)[None, :]
    return jnp.asarray((j > s).astype(np.float32), jnp.bfloat16)


def _flash_step(s, valid, vt_tile, m_ref, l_ref, acc_ref, idx):
    s = jnp.where(valid, s, NEG_INF)
    m_old = m_ref[idx]
    m_new = jnp.maximum(m_old, jnp.max(s, axis=-1, keepdims=True))
    alpha = jnp.exp(m_old - m_new)
    p = jnp.where(valid, jnp.exp(s - m_new), 0.0)
    l_ref[idx] = alpha * l_ref[idx] + jnp.sum(p, axis=-1, keepdims=True)
    acc_ref[idx] = alpha * acc_ref[idx] + _dot_nt(p.astype(vt_tile.dtype), vt_tile)
    m_ref[idx] = m_new


def _flash_init(m_ref, l_ref, acc_ref):
    m_ref[...] = jnp.full(m_ref.shape, NEG_INF, jnp.float32)
    l_ref[...] = jnp.zeros(l_ref.shape, jnp.float32)
    acc_ref[...] = jnp.zeros(acc_ref.shape, jnp.float32)


def _nsa_body(q_ref, kc_ref, vct_ref, ks_ref, vst_ref, kw_ref, vwt_ref, ng_ref, ovt_ref, o_ref,
              mem_ref, out_ref, m_ref, l_ref, acc_ref, *, tq, tk, n_cmp):
    i = pl.program_id(1)
    n_chunk = kc_ref.shape[0]
    q_pos = i * tq + _iota((1, tq), 1)
    gates = ng_ref[...].T

    nrow = _iota((n_chunk, 1), 0)
    d_cmp = q_pos - (nrow * CMP_STRIDE + (CMP_LEN - 1))
    valid_cmp = (d_cmp >= 0) & (nrow < n_cmp)
    dist_cmp = d_cmp.astype(jnp.float32)
    kc = kc_ref[...]
    vct = vct_ref[...]
    ps = jnp.zeros((LANES, tq), jnp.float32)
    for g in range(NSA_GROUPS):
        psum = jnp.zeros((n_chunk, tq), jnp.float32)
        for r in range(NSA_REP):
            h = g * NSA_REP + r
            s = _dot_nt(kc, q_ref[h]) - _SLOPES[h] * dist_cmp
            s = jnp.where(valid_cmp, s, NEG_INF)
            e = jnp.where(valid_cmp, jnp.exp(s - jnp.max(s, axis=0, keepdims=True)), 0.0)
            p = e / jnp.maximum(jnp.sum(e, axis=0, keepdims=True), 1e-30)
            psum = psum + p
            out_ref[h] = gates[3 * h:3 * h + 1, :] * _dot(vct, p.astype(vct.dtype))
        hi = psum.astype(jnp.bfloat16)
        lo = (psum - hi.astype(jnp.float32)).astype(jnp.bfloat16)
        ps = ps + _dot(ovt_ref[g], hi) + _dot(ovt_ref[g], lo)

    sub = _iota((LANES, 1), 0)
    blk = sub & (SEL_LEN - 1)
    cur = q_pos >> 6
    valid_blk = blk <= cur
    forced = (blk == 0) | (blk == cur) | (blk == cur - 1)
    score = jnp.where(valid_blk, ps + jnp.where(forced, FORCE_BONUS, 0.0), NEG_INF)
    low_half = sub < SEL_LEN
    rank = jnp.zeros((LANES, tq), jnp.float32)
    for c in range(SEL_LEN):
        other = jnp.where(low_half, score[c:c + 1, :], score[SEL_LEN + c:SEL_LEN + c + 1, :])
        beats = (other > score) | ((other == score) & (c < blk))
        rank = rank + jnp.where(beats, 1.0, 0.0)
    top_k = min(SEL_TOPK, ks_ref.shape[0] // SEL_LEN)
    unpicked = jnp.where((rank < top_k) & valid_blk, 0.0, NEG_INF).T.astype(mem_ref.dtype)
    mem_ref[...] = jnp.concatenate([unpicked] * NSA_REP, axis=0)

    wide = NSA_REP * tq
    col = _iota((1, wide), 1)
    rel = (col & (tq - 1)) - _iota((tk, 1), 0)
    rel_f = rel.astype(jnp.float32)
    causal = rel >= 0
    key_hi = jnp.where(_iota((tk, 1), 0) >= SEL_LEN, 1, 0)
    elane = _iota((tk, LANES), 1)

    def run(k_ref, vt_ref, g, kt_lo, use_sel):
        slope = jnp.zeros((1, wide), jnp.float32)
        for r in range(NSA_REP):
            slope = jnp.where((col >= r * tq) & (col < (r + 1) * tq), _SLOPES[g * NSA_REP + r], slope)
        bias = -(slope * rel_f)
        q4 = q_ref[g * NSA_REP:(g + 1) * NSA_REP].reshape(wide, LANES)
        if use_sel:
            q4 = jnp.concatenate([q4, mem_ref[...]], axis=1)
        m_ref[...] = jnp.full(m_ref.shape, NEG_INF, jnp.float32)
        l_ref[...] = jnp.zeros(l_ref.shape, jnp.float32)
        acc_ref[...] = jnp.zeros(acc_ref.shape, jnp.float32)

        def fetch(kt):
            off = pl.multiple_of(kt * tk, tk)
            k_tile = k_ref[pl.ds(off, tk), :]
            if use_sel:
                onehot = jnp.where(elane == g * SEL_LEN + kt * (tk // SEL_LEN) + key_hi, 1.0, 0.0)
                k_tile = jnp.concatenate([k_tile, onehot.astype(k_tile.dtype)], axis=1)
            return (_dot_nt(k_tile, q4),)

        def apply(pending):
            p, alpha, kt = pending
            off = pl.multiple_of(kt * tk, tk)
            acc_ref[...] = alpha * acc_ref[...] + _dot(vt_ref[:, pl.ds(off, tk)], p)

        def update(s, valid, kt):
            s = s + bias
            if valid is not None:
                s = jnp.where(valid, s, NEG_INF)
            shift = slope * ((kt - i) * tk).astype(jnp.float32)
            m_old = m_ref[...]
            m_new = jnp.maximum(m_old, jnp.max(s, axis=0, keepdims=True) + shift)
            alpha = jnp.exp(m_old - m_new)
            p = jnp.exp(s - (m_new - shift))
            l_ref[...] = alpha * l_ref[...] + jnp.sum(p, axis=0, keepdims=True)
            m_ref[...] = m_new
            return p.astype(vt_ref.dtype), alpha, kt

        def mask(kt, diag):
            if use_sel:
                return causal if diag else None
            if diag:
                return causal
            return rel < WINDOW - (i - kt) * tk

        first = fetch(i)
        pending = update(first[0], mask(i, True), i)
        n_fetch = len(first)

        def step(t, carry):
            kt = i - t
            nxt = fetch(jnp.maximum(kt - 1, 0))
            apply(carry[n_fetch:])
            pending = update(carry[0], mask(kt, False), kt)
            return nxt + pending

        out = lax.fori_loop(1, i - kt_lo + 1, step, fetch(jnp.maximum(i - 1, 0)) + pending)
        apply(out[n_fetch:])

    kt_win = jnp.maximum(i * tq - (WINDOW - 1), 0) // tk
    for g in range(NSA_GROUPS):
        for branch, (k_ref, vt_ref, lo, use_sel) in enumerate(
                [(ks_ref, vst_ref, 0, True), (kw_ref, vwt_ref, kt_win, False)]):
            run(k_ref, vt_ref, g, lo, use_sel)
            o = acc_ref[...] / jnp.maximum(l_ref[...], 1e-30)
            for r in range(NSA_REP):
                h = g * NSA_REP + r
                gate = gates[3 * h + 1 + branch:3 * h + 2 + branch, :]
                out_ref[h] = out_ref[h] + gate * o[:, r * tq:(r + 1) * tq]
    lane = _iota((tq, LANES), 1)
    for h in range(NSA_HEADS):
        keep = (lane >= SEL_LEN) == (h >= NSA_REP)
        o_ref[:, h * LANES:(h + 1) * LANES] = jnp.where(keep, out_ref[h].T, 0.0).astype(o_ref.dtype)


def _nsa_prompt(qn, kvb, winb, kc, vc, ng, batch, seq):
    tq = tk = 128
    n_chunk = seq // CMP_STRIDE
    n_cmp = (seq - CMP_LEN) // CMP_STRIDE + 1
    assert seq % tq == 0 and seq // SEL_LEN <= SEL_LEN
    ovt = jnp.swapaxes(_overlap_consts(n_chunk, SEL_LEN), 1, 2)
    kv3 = kvb.reshape(batch, seq, KV_COLS)
    win3 = winb.reshape(batch, seq, WIN_COLS)
    vct = jnp.swapaxes(vc, 1, 2)
    vst = jnp.swapaxes(kv3[:, :, BLK_SEL_V * LANES:(BLK_SEL_V + 1) * LANES], 1, 2)
    vwt = jnp.swapaxes(win3[:, :, LANES:2 * LANES], 1, 2)

    def keys_blk(c):
        return pl.BlockSpec((None, seq, LANES), lambda b, i: (b, 0, c))

    def vals_blk(n):
        return pl.BlockSpec((None, LANES, n), lambda b, i: (b, 0, 0))

    nq = seq // tq
    return pl.pallas_call(
        functools.partial(_nsa_body, tq=tq, tk=tk, n_cmp=n_cmp),
        grid=(batch, nq),
        in_specs=[pl.BlockSpec((NSA_HEADS, tq, LANES), lambda b, i: (0, b * nq + i, 0)),
                  pl.BlockSpec((None, n_chunk, LANES), lambda b, i: (b, 0, 0)), vals_blk(n_chunk),
                  keys_blk(BLK_SEL_K), vals_blk(seq),
                  keys_blk(0), vals_blk(seq),
                  pl.BlockSpec((tq, LANES), lambda b, i: (b * nq + i, 0)),
                  _full(ovt)],
        out_specs=pl.BlockSpec((tq, QN_PAD_COLS), lambda b, i: (b * nq + i, 0)),
        out_shape=jax.ShapeDtypeStruct((batch * seq, QN_PAD_COLS), jnp.bfloat16),
        scratch_shapes=[pltpu.VMEM((NSA_REP * tq, LANES), jnp.bfloat16),
                        pltpu.VMEM((NSA_HEADS, LANES, tq), jnp.float32),
                        pltpu.VMEM((1, NSA_REP * tq), jnp.float32),
                        pltpu.VMEM((1, NSA_REP * tq), jnp.float32),
                        pltpu.VMEM((LANES, NSA_REP * tq), jnp.float32)],
        compiler_params=_params("parallel", "parallel"),
        name="nsa",
    )(qn, kc, vct, kv3, vst, win3, vwt, ng, ovt)


def _softplus(z):
    return jnp.maximum(z, 0.0) + jnp.log1p(jnp.exp(-jnp.abs(z)))


EXP_UNDERFLOW = -104.0


def _sb_body(q_ref, k_ref, vt_ref, ut_ref, o_ref, *, tq, tk):
    i = pl.program_id(2)
    lane = _iota((tq, LANES), 1)
    wide = 2 * tq
    strictly_past = ((_iota((1, wide), 1) & (tq - 1)) - _iota((tk, 1), 0)) > 0
    q2 = q_ref[...].astype(jnp.float32)
    qq = jnp.concatenate([jnp.where((lane >= HEAD_DIM) == (e == 1), q2, 0.0).astype(jnp.bfloat16)
                          for e in range(2)], axis=0)
    ut = ut_ref[...]

    def fetch(kt):
        return _dot_nt(k_ref[pl.ds(pl.multiple_of(kt * tk, tk), tk), :], qq)

    def update(z, kt, rest_later, acc, past):
        sp = _softplus(z)
        log_keep = -sp if past is None else jnp.where(past, -sp, 0.0)
        hi = log_keep.astype(jnp.bfloat16)
        lo = (log_keep - hi.astype(jnp.float32)).astype(jnp.bfloat16)
        rest2 = _dot(ut, jnp.concatenate([hi, lo], axis=1))
        rest = rest2[:, :wide] + rest2[:, wide:]
        a = jnp.exp((z - sp) + rest + rest_later)
        if past is not None:
            a = jnp.where(past, a, 0.0)
        acc = acc + _dot(vt_ref[:, pl.ds(pl.multiple_of(kt * tk, tk), tk)], a.astype(vt_ref.dtype))
        return rest_later + (rest[0:1, :] + log_keep[0:1, :]), acc

    rest_later, acc = update(fetch(i), i, jnp.zeros((1, wide), jnp.float32),
                             jnp.zeros((LANES, wide), jnp.float32), strictly_past)

    def cond(c):
        return (c[0] <= i) & (c[1] > EXP_UNDERFLOW)

    def step(c):
        t, _, z, rest_later, acc = c
        kt = i - t
        z_next = fetch(jnp.maximum(kt - 1, 0))
        rest_later, acc = update(z, kt, rest_later, acc, None)
        return t + 1, jnp.max(rest_later), z_next, rest_later, acc

    out = lax.while_loop(cond, step, (jnp.int32(1), jnp.max(rest_later), fetch(jnp.maximum(i - 1, 0)),
                                      rest_later, acc))
    acc = out[4]
    sub = _iota((LANES, 1), 0)
    o_ref[...] = jnp.where(sub < HEAD_DIM, acc[:, :tq], acc[:, tq:]).T.astype(o_ref.dtype)


def _sb_prompt(qs, kvb, batch, seq):
    tq = tk = 128
    assert seq % tq == 0
    kv3 = kvb.reshape(batch, seq, KV_COLS)
    nq = seq // tq
    n_pair = SB_HEADS // 2
    ut = _upper_tri().T
    vt = jnp.swapaxes(kv3[:, :, BLK_SB_V * LANES:], 1, 2)
    return pl.pallas_call(
        functools.partial(_sb_body, tq=tq, tk=tk),
        grid=(batch, n_pair, nq),
        in_specs=[pl.BlockSpec((tq, LANES), lambda b, p, i: (b * nq + i, p)),
                  pl.BlockSpec((None, seq, LANES), lambda b, p, i: (b, 0, BLK_SB_K + p)),
                  pl.BlockSpec((None, LANES, seq), lambda b, p, i: (b, p, 0)),
                  _full(ut)],
        out_specs=pl.BlockSpec((tq, LANES), lambda b, p, i: (b * nq + i, p)),
        out_shape=jax.ShapeDtypeStruct((batch * seq, Q_SB_COLS), jnp.bfloat16),
        compiler_params=_params("parallel", "parallel", "parallel"),
        name="sb",
    )(qs, kv3, vt, ut)


def _sample_a_body(pt_ref, *refs, page, n_pages, pps):
    del pt_ref
    pool_refs = refs[:pps]
    q_ref, w1s_ref, w2s_ref, pe_ref, w1f_ref, ov_ref, ocmp_ref, mem_ref, past_ref = refs[pps:]
    j = pl.program_id(1)
    for k in range(pps):
        rows = pl.ds(pl.multiple_of((j * pps + k) * page, page), page)
        for z in range(2):
            pair = pool_refs[k][z * NSA_GROUPS:(z + 1) * NSA_GROUPS].reshape(NSA_GROUPS * HEAD_DIM, page)
            past_ref[z, rows, :] = pair.T

    @pl.when(j == n_pages // pps - 1)
    def _():
        past_len = n_pages * page
        n_chunk = past_len // CMP_STRIDE
        n_cmp = (past_len + 1 - CMP_LEN) // CMP_STRIDE + 1
        n_past_blk = past_len // SEL_LEN
        kc, vc = _compress_core((past_ref.at[0], past_ref.at[1]), n_chunk, w1s_ref, w2s_ref, pe_ref, w1f_ref)
        kc = kc.astype(jnp.bfloat16)
        vc = vc.astype(jnp.bfloat16)
        q8 = q_ref[...]
        ncol = _iota((NSA_HEADS, n_chunk), 1)
        d = past_len - (ncol * CMP_STRIDE + (CMP_LEN - 1))
        valid = (d >= 0) & (ncol < n_cmp)
        hrow = _iota((NSA_HEADS, 1), 0)
        s = _dot_nt(q8, kc) - _slope_col() * d.astype(jnp.float32)
        s = jnp.where(valid, s, NEG_INF)
        e = jnp.where(valid, jnp.exp(s - jnp.max(s, axis=-1, keepdims=True)), 0.0)
        p = e / jnp.maximum(jnp.sum(e, axis=-1, keepdims=True), 1e-30)
        ocmp_ref[...] = _dot(p.astype(vc.dtype), vc)
        psum = jnp.zeros_like(p)
        for g in range(NSA_GROUPS):
            pg = jnp.sum(p[g * NSA_REP:(g + 1) * NSA_REP], axis=0, keepdims=True)
            psum = jnp.where((hrow >= g * NSA_REP) & (hrow < (g + 1) * NSA_REP), pg, psum)
        ps = _dot_split(psum, ov_ref[0])
        blk = _iota((NSA_HEADS, LANES), 1)
        forced = (blk == 0) | (blk == n_past_blk - 1)
        score = jnp.where(blk < n_past_blk, ps + jnp.where(forced, FORCE_BONUS, 0.0), NEG_INF)
        eye = _iota((LANES, LANES), 0) == _iota((LANES, LANES), 1)
        before = _iota((LANES, LANES), 0) < _iota((LANES, LANES), 1)
        for g in range(NSA_GROUPS):
            srow = score[g * NSA_REP:g * NSA_REP + 1, :]
            scol = jnp.sum(jnp.where(eye, srow, 0.0), axis=1, keepdims=True)
            beats = (scol > srow) | ((scol == srow) & before)
            rank = jnp.sum(jnp.where(beats, 1.0, 0.0), axis=0, keepdims=True)
            top_k = min(SEL_TOPK, n_past_blk + 1)
            member = jnp.where((rank < top_k - 1) & (blk[0:1] < n_past_blk), 1.0, 0.0)
            mem_ref[g * NSA_REP:(g + 1) * NSA_REP, :] = jnp.broadcast_to(member, (NSA_REP, LANES))


def _pages_per_step(n_pages, want):
    while n_pages % want:
        want //= 2
    return want


def _sample_a(pool4, pt_flat, q8, cw, dec, n_pages, page):
    past_len = n_pages * page
    n_chunk = past_len // CMP_STRIDE
    assert past_len // SEL_LEN <= LANES and past_len % SEL_LEN == 0
    ov = _overlap_consts(n_chunk, LANES)
    o8 = jax.ShapeDtypeStruct((dec, NSA_HEADS, LANES), jnp.float32)
    pps = _pages_per_step(n_pages, 8)

    def const(a):
        nd = a.ndim
        return pl.BlockSpec(a.shape, lambda b, j, pt: (0,) * nd)

    def page_spec(k):
        return pl.BlockSpec((None, 8, HEAD_DIM, page), lambda b, j, pt: (pt[b * n_pages + j * pps + k], 0, 0, 0))

    grid_spec = pltpu.PrefetchScalarGridSpec(
        num_scalar_prefetch=1,
        grid=(dec, n_pages // pps),
        in_specs=[page_spec(k) for k in range(pps)]
                 + [pl.BlockSpec((None, NSA_HEADS, LANES), lambda b, j, pt: (b, 0, 0))]
                 + [const(a) for a in cw] + [const(ov)],
        out_specs=[pl.BlockSpec((None, NSA_HEADS, LANES), lambda b, j, pt: (b, 0, 0))] * 2,
        scratch_shapes=[pltpu.VMEM((2, past_len, LANES), jnp.float32)],
    )
    return pl.pallas_call(
        functools.partial(_sample_a_body, page=page, n_pages=n_pages, pps=pps),
        grid_spec=grid_spec,
        out_shape=[o8, o8],
        compiler_params=_params("parallel", "arbitrary"),
        name="sample_a",
    )(pt_flat, *([pool4] * pps), q8, *cw, ov)


def _sample_b_body(pt_ref, *refs, page, n_pages, wbuf, pps):
    del pt_ref
    pool_refs = refs[:pps]
    (qn_ref, qs_ref, kvn_ref, winn_ref, cw_ref, mem_ref, ocmp_ref, gate_ref, u_ref,
     oa_ref, ob_ref, wout_ref, m_ref, l_ref, acc_ref, rest_ref, sbacc_ref) = refs[pps:]
    jj = pl.program_id(1)
    n_steps = n_pages // pps
    past_len = n_pages * page
    q8 = qn_ref[...]
    hrow = _iota((NSA_HEADS, 1), 0)
    slope = _slope_col()
    lane8 = _iota((NSA_HEADS, LANES), 1)
    own_half = (lane8 >= HEAD_DIM) == (hrow >= NSA_REP)

    def new_row(ref, c):
        return ref[:, c * LANES:(c + 1) * LANES].astype(jnp.bfloat16).astype(jnp.float32)

    @pl.when(jj == 0)
    def _():
        qf = q8.astype(jnp.float32)
        m_ref[0] = jnp.sum(qf * new_row(kvn_ref, BLK_SEL_K), axis=-1, keepdims=True)
        l_ref[0] = jnp.ones((NSA_HEADS, 1), jnp.float32)
        acc_ref[0] = jnp.broadcast_to(new_row(kvn_ref, BLK_SEL_V), (NSA_HEADS, LANES))
        rest_ref[...] = jnp.zeros(rest_ref.shape, jnp.float32)
        sbacc_ref[...] = jnp.zeros(sbacc_ref.shape, jnp.float32)

    qs = qs_ref[...]
    lane_sb = _iota((SB_HEADS, Q_SB_COLS), 1)
    head_sb = _iota((SB_HEADS, Q_SB_COLS), 0)
    own_sb = (lane_sb >> 6) == head_sb
    qbd = jnp.where(own_sb, qs.astype(jnp.float32), 0.0).astype(jnp.bfloat16)
    member = mem_ref[...].astype(jnp.bfloat16)
    keys = pps * page
    def rows_t(blk_lo, blk_hi):
        r0, r1 = blk_lo * LANES // HEAD_DIM, blk_hi * LANES // HEAD_DIM
        return jnp.concatenate([pool_refs[k][r0:r1].reshape((r1 - r0) * HEAD_DIM, page) for k in range(pps)],
                               axis=1).astype(jnp.bfloat16)

    lane_k = _iota((1, keys), 1)
    tok = lane_k & (page - 1)
    page_of = n_pages - 1 - (jj * pps + (lane_k >> 7))
    d = past_len - (page_of * page + tok)
    expand = jnp.where(_iota((LANES, keys), 0) == page_of * (page // SEL_LEN) + (tok >> 6), 1.0, 0.0)
    picked = _dot(member, expand.astype(jnp.bfloat16))
    s = _dot(q8, rows_t(BLK_SEL_K, BLK_SEL_V)) - slope * d.astype(jnp.float32)
    _flash_step(s, picked > 0.5, rows_t(BLK_SEL_V, BLK_SB_K), m_ref, l_ref, acc_ref, 0)

    @pl.when(jnp.max(rest_ref[...]) > EXP_UNDERFLOW)
    def _():
        z = _dot(qbd, rows_t(BLK_SB_K, BLK_SB_V))
        sp = _softplus(z)
        log_keep = -sp
        rest = _dot_split(log_keep, u_ref[...])
        a = jnp.exp((z - sp) + rest + rest_ref[...])
        sbacc_ref[...] = sbacc_ref[...] + _dot_nt(a.astype(jnp.bfloat16), rows_t(BLK_SB_V, KV_COLS // LANES))
        oldest = (pps - 1) * page
        rest_ref[...] = rest_ref[...] + (rest[:, oldest:oldest + 1] + log_keep[:, oldest:oldest + 1])

    @pl.when(jj == n_steps - 1)
    def _():
        o_sel = acc_ref[0] / jnp.maximum(l_ref[0], 1e-30)
        cw = cw_ref[...]
        k_w = cw[:, 0:LANES].astype(jnp.bfloat16)
        v_w = cw[:, LANES:2 * LANES].astype(jnp.bfloat16)
        wcol = _iota((NSA_HEADS, wbuf), 1)
        dw = wbuf - wcol
        valid_w = (dw < WINDOW) & (past_len - dw >= 0)
        s_w = jnp.where(valid_w, _dot_nt(q8, k_w) - slope * dw.astype(jnp.float32), NEG_INF)
        s_n = jnp.sum(q8.astype(jnp.float32) * new_row(winn_ref, 0), axis=-1, keepdims=True)
        m_w = jnp.maximum(jnp.max(s_w, axis=-1, keepdims=True), s_n)
        p_w = jnp.where(valid_w, jnp.exp(s_w - m_w), 0.0)
        p_n = jnp.exp(s_n - m_w)
        l_w = jnp.sum(p_w, axis=-1, keepdims=True) + p_n
        o_win = (_dot(p_w.astype(jnp.bfloat16), v_w)
                 + p_n.astype(jnp.bfloat16).astype(jnp.float32) * new_row(winn_ref, 1)) / jnp.maximum(l_w, 1e-30)
        gate = gate_ref[...]
        o_a = gate[:, 0:1] * ocmp_ref[...] + gate[:, 1:2] * o_sel + gate[:, 2:3] * o_win
        oa_ref[...] = jnp.where(own_half, o_a, 0.0)
        ob_ref[...] = jnp.sum(jnp.where(own_sb, sbacc_ref[...], 0.0), axis=0, keepdims=True)
        wrow = _iota((wbuf, 1), 0)
        wout_ref[...] = jnp.where(wrow == wbuf - 1, winn_ref[...], pltpu.roll(cw, wbuf - 1, 0))


def _sample_b(pool_rows, pt_flat, q8, qs, kv32, win32, cache_win2, member, ocmp, gate, dec, n_pages, page):
    wbuf = cache_win2.shape[1]
    assert page == LANES
    pps = _pages_per_step(n_pages, 8)
    slot, tok = np.divmod(np.arange(pps * page), page)
    later = (slot[:, None] < slot[None, :]) | ((slot[:, None] == slot[None, :]) & (tok[:, None] > tok[None, :]))
    u = jnp.asarray(later.astype(np.float32), jnp.bfloat16)

    def per_seq(*shape):
        nd = len(shape)
        return pl.BlockSpec((None,) + shape, lambda b, jj, pt: (b,) + (0,) * nd)

    def page_spec(k):
        return pl.BlockSpec((None, KV_ROWS, HEAD_DIM, page),
                            lambda b, jj, pt: (pt[b * n_pages + n_pages - 1 - (jj * pps + k)], 0, 0, 0))

    grid_spec = pltpu.PrefetchScalarGridSpec(
        num_scalar_prefetch=1,
        grid=(dec, n_pages // pps),
        in_specs=[page_spec(k) for k in range(pps)] + [
                  per_seq(NSA_HEADS, LANES), per_seq(1, Q_SB_COLS), per_seq(1, KV_COLS), per_seq(1, WIN_COLS),
                  per_seq(wbuf, WIN_COLS), per_seq(NSA_HEADS, LANES), per_seq(NSA_HEADS, LANES),
                  per_seq(NSA_HEADS, 3),
                  pl.BlockSpec(u.shape, lambda b, jj, pt: (0, 0))],
        out_specs=[per_seq(NSA_HEADS, LANES), per_seq(1, Q_SB_COLS), per_seq(wbuf, WIN_COLS)],
        scratch_shapes=[pltpu.VMEM((1, NSA_HEADS, 1), jnp.float32), pltpu.VMEM((1, NSA_HEADS, 1), jnp.float32),
                        pltpu.VMEM((1, NSA_HEADS, LANES), jnp.float32), pltpu.VMEM((SB_HEADS, 1), jnp.float32),
                        pltpu.VMEM((SB_HEADS, Q_SB_COLS), jnp.float32)],
    )
    return pl.pallas_call(
        functools.partial(_sample_b_body, page=page, n_pages=n_pages, wbuf=wbuf, pps=pps),
        grid_spec=grid_spec,
        out_shape=[jax.ShapeDtypeStruct((dec, NSA_HEADS, LANES), jnp.float32),
                   jax.ShapeDtypeStruct((dec, 1, Q_SB_COLS), jnp.float32),
                   jax.ShapeDtypeStruct((dec, wbuf, WIN_COLS), jnp.float32)],
        compiler_params=_params("parallel", "arbitrary"),
        name="sample_b",
    )(pt_flat, *([pool_rows] * pps), q8, qs, kv32, win32, cache_win2, member, ocmp, gate, u)


def _post_body(x_ref, oa_ref, ob_ref, mg_ref, wa_ref, wb_ref, wo_ref, g_ref, y_ref):
    d = x_ref.shape[1]
    u_a = _dot(oa_ref[...].astype(jnp.bfloat16), wa_ref[...])
    u_b = _dot(ob_ref[...].astype(jnp.bfloat16), wb_ref[...])
    mixed = mg_ref[:, 0:d].astype(jnp.float32) * u_a + mg_ref[:, d:2 * d].astype(jnp.float32) * u_b
    y_ref[...] = x_ref[...] + _rms(_dot(mixed.astype(jnp.bfloat16), wo_ref[...]), g_ref[...])


def _post(x2, oa, ob, mg, wa, wb, wo, g):
    m, d = x2.shape
    tm = min(256, m)

    def rows(c):
        return pl.BlockSpec((tm, c), lambda i: (i, 0))

    return pl.pallas_call(
        _post_body,
        grid=(m // tm,),
        in_specs=[rows(d), rows(oa.shape[1]), rows(ob.shape[1]), rows(mg.shape[1]),
                  _full(wa), _full(wb), _full(wo), _full(g)],
        out_specs=rows(d),
        out_shape=jax.ShapeDtypeStruct((m, d), jnp.float32),
        compiler_params=_params("parallel"),
        name="post",
    )(x2, oa, ob, mg, wa, wb, wo, g)


def _ffn_body(x_ref, prev_ref, g2_ref, g3_ref, wup_ref, cw_ref, cb_ref, wdn_ref, y_ref, st_ref, carry_ref,
              *, tm, d_ff, fc, per_row):
    t = pl.program_id(1)
    x = x_ref[...]
    h = _rms(x, g2_ref[...]).astype(jnp.bfloat16)
    if not per_row:
        @pl.when(t == 0)
        def _():
            carry_ref[...] = prev_ref[...]
    row = _iota((tm, 1), 0)
    acc = jnp.zeros((tm, x.shape[1]), jnp.float32)
    for c0 in range(0, d_ff, fc):
        halves = []
        for base in (c0, d_ff + c0):
            cols = slice(base, base + fc)
            u = _dot(h, wup_ref[:, cols])
            if per_row:
                u1 = prev_ref[1, :, cols]
                u2 = prev_ref[0, :, cols]
                st_ref[0, :, cols] = u1
                st_ref[1, :, cols] = u
            else:
                p1 = carry_ref[1:2, cols]
                p2 = carry_ref[0:1, cols]
                u1 = jnp.where(row < 1, p1, pltpu.roll(u, 1, 0))
                u2 = jnp.where(row < 1, p2, jnp.where(row < 2, p1, pltpu.roll(u, 2, 0)))
                carry_ref[:, cols] = u[tm - 2:tm, :]
                st_ref[:, cols] = u[tm - 2:tm, :]
            halves.append(cb_ref[:, cols] + u2 * cw_ref[0:1, cols] + u1 * cw_ref[1:2, cols] + u * cw_ref[2:3, cols])
        act = (_gelu_tanh(halves[0]) * halves[1]).astype(jnp.bfloat16)
        acc = acc + _dot(act, wdn_ref[c0:c0 + fc, :])
    y_ref[...] = x + _rms(acc, g3_ref[...])


def _ffn(x2, prev, g2, g3, wup, cw, cb, wdn, batch, seq, per_row):
    m, d = x2.shape
    d_ff = wdn.shape[0]
    fc = 256
    assert d_ff % fc == 0
    if per_row:
        tm, nb, nt = m, 1, 1
        prev_spec = pl.BlockSpec((2, tm, 2 * d_ff), lambda b, t: (0, 0, 0))
        st_spec = pl.BlockSpec((2, tm, 2 * d_ff), lambda b, t: (0, 0, 0))
        st_shape = jax.ShapeDtypeStruct((2, m, 2 * d_ff), jnp.float32)
    else:
        tm = min(256, seq)
        nb, nt = batch, seq // tm
        prev_spec = pl.BlockSpec((None, 2, 2 * d_ff), lambda b, t: (b, 0, 0))
        st_spec = pl.BlockSpec((None, 2, 2 * d_ff), lambda b, t: (b, 0, 0))
        st_shape = jax.ShapeDtypeStruct((batch, 2, 2 * d_ff), jnp.float32)
    return pl.pallas_call(
        functools.partial(_ffn_body, tm=tm, d_ff=d_ff, fc=fc, per_row=per_row),
        grid=(nb, nt),
        in_specs=[pl.BlockSpec((tm, d), lambda b, t: (b * nt + t, 0)), prev_spec,
                  _full(g2), _full(g3), _full(wup), _full(cw), _full(cb), _full(wdn)],
        out_specs=[pl.BlockSpec((tm, d), lambda b, t: (b * nt + t, 0)), st_spec],
        out_shape=[jax.ShapeDtypeStruct((m, d), jnp.float32), st_shape],
        scratch_shapes=[pltpu.VMEM((2, 2 * d_ff), jnp.float32)],
        compiler_params=_params("parallel", "arbitrary"),
        name="ffn",
    )(x2, prev, g2, g3, wup, cw, cb, wdn)


def _branch_a_weights(w_branch_a):
    d = w_branch_a.shape[1]
    w = w_branch_a.reshape(NSA_GROUPS, NSA_REP, 1, HEAD_DIM, d)
    half = jnp.eye(NSA_GROUPS, dtype=w.dtype)
    return (w * half[:, None, :, None, None]).reshape(QN_PAD_COLS, d).astype(jnp.bfloat16)


def kernel(x_prompt, x_sample, cache_kv, cache_win, state_conv, page_table, norm_g, w_in, cmp_w1, cmp_w2, cmp_pe,
           w_branch_a, w_branch_b, w_out, ffn_w_up, ffn_conv_w, ffn_conv_b, ffn_w_down):
    depth = norm_g.shape[0]
    assert depth == 1, "one layer per call: outputs of deeper stacks are not wired"
    batch, seq, d_model = x_prompt.shape
    dec, dec_seq, _ = x_sample.shape
    assert dec_seq == 1
    n_pool, page = cache_kv.shape[1], cache_kv.shape[2]
    n_pages = page_table.shape[1]
    wbuf = cache_win.shape[2]
    d_ff = ffn_w_down.shape[1]
    l = 0

    g = norm_g[l].reshape(4, 1, d_model)
    wp = _proj_weights(w_in[l], d_model)
    cw = _compress_weights(cmp_w1[l], cmp_w2[l], cmp_pe[l])
    wa = _branch_a_weights(w_branch_a[l])
    wb = w_branch_b[l].astype(jnp.bfloat16)
    wo = w_out[l].astype(jnp.bfloat16)
    wup = ffn_w_up[l].astype(jnp.bfloat16)
    wdn = ffn_w_down[l].astype(jnp.bfloat16)
    conv_w = ffn_conv_w[l]
    conv_b = ffn_conv_b[l].reshape(1, 2 * d_ff)

    xp = x_prompt.reshape(batch * seq, d_model)
    qn, kv32, kvb, win32, winb, qs, ng, mg = _proj(xp, g[0], wp)
    kc, vc = _compress_prompt(kv32, cw, batch, seq)
    oa = _nsa_prompt(qn, kvb, winb, kc, vc, ng, batch, seq)
    ob = _sb_prompt(qs, kvb, batch, seq)
    x1 = _post(xp, oa, ob, mg, wa, wb, wo, g[1])
    conv0 = jnp.zeros((batch, CONV_W - 1, 2 * d_ff), jnp.float32)
    yp, conv_p = _ffn(x1, conv0, g[2], g[3], wup, conv_w, conv_b, wdn, batch, seq, per_row=False)
    kv_p = kv32.reshape(1, batch, seq, KV_ROWS, HEAD_DIM)
    win_seq = win32.reshape(batch, seq, WIN_ROWS, HEAD_DIM)
    win_p = jnp.pad(win_seq, ((0, 0), (wbuf, 0), (0, 0), (0, 0)))[:, -wbuf:][None]

    xs = x_sample.reshape(dec, d_model)
    qn_s, kv32_s, _, win32_s, _, qs_s, ng_s, mg_s = _proj(xs, g[0], wp)
    pool4 = jnp.transpose(cache_kv[l], (0, 2, 3, 1))
    pool_rows = pool4
    pt_flat = page_table.reshape(dec * n_pages)
    q8 = jnp.swapaxes(qn_s, 0, 1)
    ocmp, member = _sample_a(pool4, pt_flat, q8, cw, dec, n_pages, page)
    gate = ng_s[:, :NSA_GATE_COLS].reshape(dec, NSA_HEADS, 3)
    oa_s, ob_s, win_s = _sample_b(pool_rows, pt_flat, q8, qs_s.reshape(dec, 1, Q_SB_COLS),
                                  kv32_s.reshape(dec, 1, KV_COLS), win32_s.reshape(dec, 1, WIN_COLS),
                                  cache_win[l].reshape(dec, wbuf, WIN_COLS), member, ocmp, gate, dec, n_pages, page)
    x1_s = _post(xs, oa_s.reshape(dec, QN_PAD_COLS), ob_s.reshape(dec, Q_SB_COLS), mg_s, wa, wb, wo, g[1])
    prev_s = jnp.swapaxes(state_conv[l], 0, 1)
    ys, conv_s = _ffn(x1_s, prev_s, g[2], g[3], wup, conv_w, conv_b, wdn, dec, 1, per_row=True)

    return (yp.reshape(batch, seq, d_model), ys.reshape(dec, 1, d_model), kv_p,
            kv32_s.reshape(1, dec, 1, KV_ROWS, HEAD_DIM), win_p,
            win_s.reshape(1, dec, wbuf, WIN_ROWS, HEAD_DIM), conv_p[None],
            jnp.swapaxes(conv_s, 0, 1)[None])
```
